```python
import math
import jax
import jax.numpy as jnp
from jax import lax
import numpy as np

D_MODEL = 4096
BATCH = 8
SEQ = 2048
DEPTH = 2
DEC_BATCH = 32
DEC_SEQ = 64
PAST_LEN = 2048

CHUNK = 64
N_MIXERS = 2
N_A_LAYERS = (DEPTH + 1) // 2
N_B_LAYERS = DEPTH // 2
A_DK = 128
A_HEADS = D_MODEL // A_DK
A_DV = D_MODEL // A_HEADS
A_BLOCK = 16
B_DK = 128
B_HEADS = D_MODEL // (2 * B_DK)
B_DV = 2 * B_DK
Q_BLOCK = 128
N_GROUPS = 4
EXPERTS_PER_GROUP = 8
N_EXPERTS = N_GROUPS * EXPERTS_PER_GROUP
TOP_K = 2
D_EXPERT = D_MODEL // 4
MOE_BLOCK = 128
PLE_DIM = 256
EPS = 1e-6

kernel_name = 'hybrid_hgrn2_diffattn_hmoe_stream_step'


def rmsnorm(x, g):
    xf = x.astype(jnp.float32)
    y = xf * lax.rsqrt(jnp.mean(xf * xf, axis=-1, keepdims=True) + EPS)
    return (y * g.astype(jnp.float32)).astype(x.dtype)


def lambda_init(layer_idx):
    return 0.8 - 0.6 * math.exp(-0.3 * layer_idx)


def hgrn2_scan(q, k, v, logf, s0):
    bsz, t, nh, _ = q.shape
    L = A_BLOCK
    nb = -(-t // L)
    pad = nb * L - t

    def prep(a):
        a = a.astype(jnp.float32)
        if pad:
            a = jnp.pad(a, ((0, 0), (0, pad), (0, 0), (0, 0)))
        return a.reshape(bsz, nb, L, nh, a.shape[-1]).transpose(1, 0, 3, 2, 4)

    causal = jnp.tril(jnp.ones((L, L), bool))[None, None, :, :, None]

    def step(S, inp):
        qb, kb, vb, gb = inp
        b = jnp.cumsum(gb, axis=2)
        rel = b[:, :, :, None, :] - b[:, :, None, :, :]
        decay = jnp.exp(jnp.where(causal, rel, -jnp.inf))
        att = jnp.einsum('bhtd,bhsd,bhtsd->bhts', qb, kb, decay)
        o = (jnp.einsum('bhts,bhsv->bhtv', att, vb)
             + jnp.einsum('bhtd,bhdv->bhtv', qb * jnp.exp(b), S))
        b_end = b[:, :, -1:, :]
        S = (jnp.exp(b_end[:, :, 0, :])[..., None] * S
             + jnp.einsum('bhsd,bhsv->bhdv', kb * jnp.exp(b_end - b), vb))
        return S, o

    S, o = lax.scan(step, s0, (prep(q), prep(k), prep(v), prep(logf)))
    o = o.transpose(1, 0, 3, 2, 4).reshape(bsz, nb * L, nh, -1)[:, :t]
    return o, S


def hgrn2_mixer(xn, w_in, w_out, gnorm, lb, s0):
    bsz, t, _ = xn.shape
    q, fl, v, g = jnp.split(xn @ w_in, 4, axis=-1)
    q = jax.nn.silu(q.astype(jnp.float32))
    f = lb + (1.0 - lb) * jax.nn.sigmoid(fl.astype(jnp.float32))
    k = 1.0 - f
    logf = jnp.log(f)
    heads = lambda a: a.reshape(bsz, t, A_HEADS, -1)
    o, s = hgrn2_scan(heads(q), heads(k), heads(v), heads(logf), s0)
    o = rmsnorm(o, gnorm).reshape(bsz, t, D_MODEL)
    o = (o * jax.nn.silu(g.astype(jnp.float32))).astype(xn.dtype)
    return o @ w_out, s


def diff_attn_block(qb, k, v, allowed, lam):
    s = jnp.einsum('bqhcd,bshcd->bhcqs', qb, k).astype(jnp.float32) * (B_DK ** -0.5)
    s = jnp.where(allowed, s, -jnp.inf)
    p = jax.nn.softmax(s, axis=-1)
    w = p[:, :, 0] - lam * p[:, :, 1]
    return jnp.einsum('bhqs,bshv->bqhv', w.astype(v.dtype), v)


def diff_attn_mixer(xn, w_in, w_out, lq1, lk1, lq2, lk2, subln, lam0, past_k, past_v):
    bsz, t, _ = xn.shape
    q, k, v = jnp.split(xn @ w_in, 3, axis=-1)
    q = q.reshape(bsz, t, B_HEADS, 2, B_DK)
    k = k.reshape(bsz, t, B_HEADS, 2, B_DK)
    v = v.reshape(bsz, t, B_HEADS, B_DV)
    lam = (jnp.exp(jnp.sum(lq1.astype(jnp.float32) * lk1.astype(jnp.float32)))
           - jnp.exp(jnp.sum(lq2.astype(jnp.float32) * lk2.astype(jnp.float32))) + lam0)
    if past_k is None:
        nqb = t // Q_BLOCK
        qblocks = jnp.moveaxis(q.reshape(bsz, nqb, Q_BLOCK, B_HEADS, 2, B_DK), 1, 0)
        starts = jnp.arange(nqb, dtype=jnp.int32) * Q_BLOCK
        k_pos = jnp.arange(t, dtype=jnp.int32)

        def one(args):
            qb, st = args
            q_pos = st + jnp.arange(Q_BLOCK, dtype=jnp.int32)
            limit = (q_pos // CHUNK + 1) * CHUNK
            return diff_attn_block(qb, k, v, k_pos[None, :] < limit[:, None], lam)

        o = lax.map(one, (qblocks, starts))
        o = jnp.moveaxis(o, 0, 1).reshape(bsz, t, B_HEADS, B_DV)
    else:
        kk = jnp.concatenate([past_k.astype(k.dtype), k], axis=1)
        vv = jnp.concatenate([past_v.astype(v.dtype), v], axis=1)
        allowed = jnp.ones((t, kk.shape[1]), bool)
        o = diff_attn_block(q, kk, vv, allowed, lam)
    o = rmsnorm(o, subln) * (1.0 - lam0)
    return o.reshape(bsz, t, D_MODEL) @ w_out, k, v


def grouped_experts(xf, expert_idx, gates, w_gate, w_up, w_down):
    n, d = xf.shape
    a = n * TOP_K
    flat_e = expert_idx.reshape(-1)
    flat_tok = jnp.repeat(jnp.arange(n, dtype=jnp.int32), TOP_K)
    flat_w = gates.reshape(-1)
    order = jnp.argsort(flat_e)
    se, stok, sw = flat_e[order], flat_tok[order], flat_w[order]
    counts = jnp.bincount(flat_e, length=N_EXPERTS)
    padded = (counts + MOE_BLOCK - 1) // MOE_BLOCK * MOE_BLOCK
    start = jnp.cumsum(counts) - counts
    pend = jnp.cumsum(padded)
    pstart = pend - padded
    dest = pstart[se] + (jnp.arange(a, dtype=jnp.int32) - start[se])
    n_blocks = -(-(a + N_EXPERTS * (MOE_BLOCK - 1)) // MOE_BLOCK)
    P = n_blocks * MOE_BLOCK
    slot_tok = jnp.zeros((P,), jnp.int32).at[dest].set(stok)
    slot_w = jnp.zeros((P,), jnp.float32).at[dest].set(sw)
    blk_start = jnp.arange(n_blocks, dtype=jnp.int32) * MOE_BLOCK
    blk_exp = jnp.minimum(jnp.searchsorted(pend, blk_start, side='right'), N_EXPERTS - 1)
    xs = xf[slot_tok].reshape(n_blocks, MOE_BLOCK, d)

    def expert_block(args):
        xb, e = args
        hdn = jax.nn.silu(xb @ w_gate[e]) * (xb @ w_up[e])
        return hdn @ w_down[e]

    yb = lax.map(expert_block, (xs, blk_exp)).reshape(P, d)
    y = jnp.zeros((n, d), jnp.float32).at[slot_tok].add(yb.astype(jnp.float32) * slot_w[:, None])
    return y.astype(xf.dtype)


def hier_moe(xn, wg, bg, we, be, w_gate, w_up, w_down):
    bsz, t, d = xn.shape
    xf = xn.reshape(-1, d)
    n = xf.shape[0]
    pg = jax.nn.softmax((xf @ wg).astype(jnp.float32) + bg.astype(jnp.float32), axis=-1)
    pg_top, g_top = lax.top_k(pg, 1)
    el = ((xf @ we).astype(jnp.float32) + be.astype(jnp.float32)).reshape(n, N_GROUPS, EXPERTS_PER_GROUP)
    el_sel = jnp.take_along_axis(el, g_top[:, :, None], axis=1)[:, 0]
    pe_top, e_top = lax.top_k(jax.nn.softmax(el_sel, axis=-1), TOP_K)
    gates = pg_top * pe_top / jnp.sum(pe_top, axis=-1, keepdims=True)
    expert_idx = g_top * EXPERTS_PER_GROUP + e_top
    return grouped_experts(xf, expert_idx, gates, w_gate, w_up, w_down).reshape(bsz, t, d)


def per_layer_embed(h, p_i, norm_g, w_up, w_gate):
    e = p_i.astype(h.dtype) @ w_up
    gate = jax.nn.sigmoid((rmsnorm(h, norm_g) @ w_gate).astype(jnp.float32))
    return (e.astype(jnp.float32) * gate).astype(h.dtype)


def trunk(x, p, s0_list, cache_k, cache_v, P):
    lb_all = jnp.cumsum(jax.nn.softmax(P['a_lb_logits'].astype(jnp.float32), axis=0), axis=0)
    h = x
    new_s, new_k, new_v = [], [], []
    for i in range(DEPTH):
        xn = rmsnorm(h, P['norm_mix'][i])
        j = i // N_MIXERS
        if i % N_MIXERS == 0:
            mix, s = hgrn2_mixer(xn, P['a_w_in'][j], P['a_w_out'][j], P['a_gnorm'][j], lb_all[j], s0_list[j])
            new_s.append(s)
        else:
            pk = None if cache_k is None else cache_k[j]
            pv = None if cache_v is None else cache_v[j]
            mix, k, v = diff_attn_mixer(xn, P['b_w_in'][j], P['b_w_out'][j], P['b_lambda_q1'][j],
                                        P['b_lambda_k1'][j], P['b_lambda_q2'][j], P['b_lambda_k2'][j],
                                        P['b_subln'][j], lambda_init(i), pk, pv)
            new_k.append(k)
            new_v.append(v)
        h = h + mix
        h = h + hier_moe(rmsnorm(h, P['norm_ffn'][i]), P['router_group_w'][i], P['router_group_b'][i],
                         P['router_expert_w'][i], P['router_expert_b'][i], P['expert_w_gate'][i],
                         P['expert_w_up'][i], P['expert_w_down'][i])
        h = h + per_layer_embed(h, p[i], P['norm_ple'][i], P['ple_w_up'][i], P['ple_w_gate'][i])
    return rmsnorm(h, P['norm_final']), jnp.stack(new_s), jnp.stack(new_k), jnp.stack(new_v)


def setup_inputs(seed: int = 0) -> dict:
    key = jax.random.key(seed)
    ks = jax.random.split(key, 32)
    f32 = jnp.float32

    def nrm(k, shape, scale=1.0):
        return jax.random.normal(k, shape, f32) * scale

    def gain(k, shape):
        return 1.0 + 0.02 * jax.random.normal(k, shape, f32)

    D = D_MODEL
    return {
        'x_prompt': nrm(ks[0], (BATCH, SEQ, D)),
        'x_sample': nrm(ks[1], (DEC_BATCH, DEC_SEQ, D)),
        'state_hgrn': nrm(ks[2], (N_A_LAYERS, DEC_BATCH, A_HEADS, A_DK, A_DV), 0.5),
        'cache_k': nrm(ks[3], (N_B_LAYERS, DEC_BATCH, PAST_LEN, B_HEADS, 2, B_DK)),
        'cache_v': nrm(ks[4], (N_B_LAYERS, DEC_BATCH, PAST_LEN, B_HEADS, B_DV)),
        'p_prompt': nrm(ks[5], (DEPTH, BATCH, SEQ, PLE_DIM)),
        'p_sample': nrm(ks[6], (DEPTH, DEC_BATCH, DEC_SEQ, PLE_DIM)),
        'norm_mix': gain(ks[7], (DEPTH, D)),
        'norm_ffn': gain(ks[8], (DEPTH, D)),
        'norm_ple': gain(ks[9], (DEPTH, D)),
        'norm_final': gain(ks[10], (D,)),
        'a_w_in': nrm(ks[11], (N_A_LAYERS, D, 4 * D), D ** -0.5),
        'a_w_out': nrm(ks[12], (N_A_LAYERS, D, D), D ** -0.5),
        'a_lb_logits': nrm(ks[13], (N_A_LAYERS + 1, D), 0.5),
        'a_gnorm': gain(ks[14], (N_A_LAYERS, A_DV)),
        'b_w_in': nrm(ks[15], (N_B_LAYERS, D, 3 * D), D ** -0.5),
        'b_w_out': nrm(ks[16], (N_B_LAYERS, D, D), D ** -0.5),
        'b_lambda_q1': nrm(ks[17], (N_B_LAYERS, B_DK), 0.1),
        'b_lambda_k1': nrm(ks[18], (N_B_LAYERS, B_DK), 0.1),
        'b_lambda_q2': nrm(ks[19], (N_B_LAYERS, B_DK), 0.1),
        'b_lambda_k2': nrm(ks[20], (N_B_LAYERS, B_DK), 0.1),
        'b_subln': gain(ks[21], (N_B_LAYERS, B_DV)),
        'router_group_w': nrm(ks[22], (DEPTH, D, N_GROUPS), D ** -0.5),
        'router_group_b': nrm(ks[23], (DEPTH, N_GROUPS), 0.01),
        'router_expert_w': nrm(ks[24], (DEPTH, D, N_EXPERTS), D ** -0.5),
        'router_expert_b': nrm(ks[25], (DEPTH, N_EXPERTS), 0.01),
        'expert_w_gate': nrm(ks[26], (DEPTH, N_EXPERTS, D, D_EXPERT), D ** -0.5),
        'expert_w_up': nrm(ks[27], (DEPTH, N_EXPERTS, D, D_EXPERT), D ** -0.5),
        'expert_w_down': nrm(ks[28], (DEPTH, N_EXPERTS, D_EXPERT, D), D_EXPERT ** -0.5),
        'ple_w_up': nrm(ks[29], (DEPTH, PLE_DIM, D), PLE_DIM ** -0.5),
        'ple_w_gate': nrm(ks[30], (DEPTH, D, D), D ** -0.5),
    }


def reference(x_prompt, x_sample, state_hgrn, cache_k, cache_v, p_prompt, p_sample,
              norm_mix, norm_ffn, norm_ple, norm_final, a_w_in, a_w_out, a_lb_logits, a_gnorm,
              b_w_in, b_w_out, b_lambda_q1, b_lambda_k1, b_lambda_q2, b_lambda_k2, b_subln,
              router_group_w, router_group_b, router_expert_w, router_expert_b,
              expert_w_gate, expert_w_up, expert_w_down, ple_w_up, ple_w_gate):
    params = dict(norm_mix=norm_mix, norm_ffn=norm_ffn, norm_ple=norm_ple, norm_final=norm_final,
                  a_w_in=a_w_in, a_w_out=a_w_out, a_lb_logits=a_lb_logits, a_gnorm=a_gnorm,
                  b_w_in=b_w_in, b_w_out=b_w_out, b_lambda_q1=b_lambda_q1, b_lambda_k1=b_lambda_k1,
                  b_lambda_q2=b_lambda_q2, b_lambda_k2=b_lambda_k2, b_subln=b_subln,
                  router_group_w=router_group_w, router_group_b=router_group_b,
                  router_expert_w=router_expert_w, router_expert_b=router_expert_b,
                  expert_w_gate=expert_w_gate, expert_w_up=expert_w_up, expert_w_down=expert_w_down,
                  ple_w_up=ple_w_up, ple_w_gate=ple_w_gate)
    s0_prompt = [jnp.zeros((x_prompt.shape[0], A_HEADS, A_DK, A_DV), jnp.float32) for _ in range(N_A_LAYERS)]
    s0_sample = [state_hgrn[j].astype(jnp.float32) for j in range(N_A_LAYERS)]
    y_prompt, s_prompt, k_prompt, v_prompt = trunk(x_prompt, p_prompt, s0_prompt, None, None, params)
    y_sample, s_sample, k_sample, v_sample = trunk(x_sample, p_sample, s0_sample, cache_k, cache_v, params)
    return (y_prompt, y_sample, s_prompt, s_sample, k_prompt, v_prompt, k_sample, v_sample)
```

```python
import functools
import math

import jax
import jax.numpy as jnp
import numpy as np
from jax import lax
from jax.experimental import pallas as pl
from jax.experimental.pallas import tpu as pltpu

F32 = jnp.float32
BF16 = jnp.bfloat16

EPS = 1e-6
HEAD = 128
STREAM_CHUNK = 64
N_GROUPS = 4
EXPERTS_PER_GROUP = 8
N_EXPERTS = N_GROUPS * EXPERTS_PER_GROUP
TOP_K = 2
ROUTER_LANES = 128
MOE_ROWS = 256
V7X_VMEM_LIMIT = 56 * 1024 * 1024


def _tile(dim, want):
    t = min(dim, want)
    while dim % t:
        t //= 2
    return t


def _params(sem):
    return pltpu.CompilerParams(dimension_semantics=sem, vmem_limit_bytes=V7X_VMEM_LIMIT)


def _split3(x):
    hi = x.astype(BF16)
    r1 = x - hi.astype(F32)
    mid = r1.astype(BF16)
    lo = (r1 - mid.astype(F32)).astype(BF16)
    return hi, mid, lo


def _rmsnorm_kernel(x_ref, g_ref, o_ref):
    x = x_ref[...]
    y = x * lax.rsqrt(jnp.mean(x * x, axis=-1, keepdims=True) + EPS)
    o_ref[...] = (y * g_ref[...]).astype(o_ref.dtype)


def _rmsnorm(x, g, out_dtype):
    n, d = x.shape
    tm = _tile(n, 512)
    return pl.pallas_call(
        _rmsnorm_kernel,
        out_shape=jax.ShapeDtypeStruct((n, d), out_dtype),
        grid=(n // tm,),
        in_specs=[pl.BlockSpec((tm, d), lambda i: (i, 0)), pl.BlockSpec((1, d), lambda i: (0, 0))],
        out_specs=pl.BlockSpec((tm, d), lambda i: (i, 0)),
        compiler_params=_params(("parallel",)),
        name="rmsnorm",
    )(x, g.reshape(1, d))


def _rmsnorm_router_kernel(x_ref, g_ref, w0_ref, w1_ref, w2_ref, o_ref, l_ref):
    x = x_ref[...]
    y = x * lax.rsqrt(jnp.mean(x * x, axis=-1, keepdims=True) + EPS)
    xn = y * g_ref[...]
    o_ref[...] = xn.astype(o_ref.dtype)
    x0, x1, x2 = _split3(xn)
    w0, w1, w2 = w0_ref[...], w1_ref[...], w2_ref[...]
    dot = functools.partial(jnp.dot, preferred_element_type=F32)
    small = dot(x0, w2) + dot(x1, w1) + dot(x2, w0)
    mid = dot(x0, w1) + dot(x1, w0)
    l_ref[...] = dot(x0, w0) + (mid + small)


def _rmsnorm_router(x, g, w_router):
    n, d = x.shape
    tm = _tile(n, 512)
    w0, w1, w2 = _split3(w_router)
    wspec = pl.BlockSpec((d, ROUTER_LANES), lambda i: (0, 0))
    return pl.pallas_call(
        _rmsnorm_router_kernel,
        out_shape=(jax.ShapeDtypeStruct((n, d), BF16), jax.ShapeDtypeStruct((n, ROUTER_LANES), F32)),
        grid=(n // tm,),
        in_specs=[pl.BlockSpec((tm, d), lambda i: (i, 0)), pl.BlockSpec((1, d), lambda i: (0, 0)),
                  wspec, wspec, wspec],
        out_specs=(pl.BlockSpec((tm, d), lambda i: (i, 0)), pl.BlockSpec((tm, ROUTER_LANES), lambda i: (i, 0))),
        compiler_params=_params(("parallel",)),
        name="rmsnorm_router",
    )(x, g.reshape(1, d), w0, w1, w2)


def _mm_kernel(x_ref, w_ref, o_ref, *, scale):
    acc = jnp.dot(x_ref[...], w_ref[...], preferred_element_type=F32)
    if scale is not None:
        acc = acc * scale
    o_ref[...] = acc.astype(o_ref.dtype)


def _mm_res_kernel(x_ref, w_ref, r_ref, o_ref):
    o_ref[...] = r_ref[...] + jnp.dot(x_ref[...], w_ref[...], preferred_element_type=F32)


def _mm_ple_kernel(x_ref, w_ref, r_ref, p_ref, wup_ref, o_ref):
    gate = jax.nn.sigmoid(jnp.dot(x_ref[...], w_ref[...], preferred_element_type=F32))
    e = jnp.dot(p_ref[...], wup_ref[...], preferred_element_type=F32)
    o_ref[...] = r_ref[...] + e * gate


def _matmul(x, w, *, out_dtype=F32, scale=None, tm=1024, tn=1024):
    m, k = x.shape
    n = w.shape[1]
    tm, tn = _tile(m, tm), _tile(n, tn)
    return pl.pallas_call(
        functools.partial(_mm_kernel, scale=scale),
        out_shape=jax.ShapeDtypeStruct((m, n), out_dtype),
        grid=(m // tm, n // tn),
        in_specs=[pl.BlockSpec((tm, k), lambda i, j: (i, 0)), pl.BlockSpec((k, tn), lambda i, j: (0, j))],
        out_specs=pl.BlockSpec((tm, tn), lambda i, j: (i, j)),
        compiler_params=_params(("parallel", "arbitrary")),
        name="matmul",
    )(x, w)


def _matmul_residual(x, w, res, *, tm=1024, tn=512):
    m, k = x.shape
    n = w.shape[1]
    tm, tn = _tile(m, tm), _tile(n, tn)
    return pl.pallas_call(
        _mm_res_kernel,
        out_shape=jax.ShapeDtypeStruct((m, n), F32),
        grid=(m // tm, n // tn),
        in_specs=[pl.BlockSpec((tm, k), lambda i, j: (i, 0)), pl.BlockSpec((k, tn), lambda i, j: (0, j)),
                  pl.BlockSpec((tm, tn), lambda i, j: (i, j))],
        out_specs=pl.BlockSpec((tm, tn), lambda i, j: (i, j)),
        compiler_params=_params(("parallel", "arbitrary")),
        name="matmul_residual",
    )(x, w, res)


def _matmul_ple(xn, w_gate, res, p, w_up, *, tm=1024, tn=512):
    m, k = xn.shape
    n = w_gate.shape[1]
    kp = p.shape[1]
    tm, tn = _tile(m, tm), _tile(n, tn)
    return pl.pallas_call(
        _mm_ple_kernel,
        out_shape=jax.ShapeDtypeStruct((m, n), F32),
        grid=(m // tm, n // tn),
        in_specs=[pl.BlockSpec((tm, k), lambda i, j: (i, 0)), pl.BlockSpec((k, tn), lambda i, j: (0, j)),
                  pl.BlockSpec((tm, tn), lambda i, j: (i, j)),
                  pl.BlockSpec((tm, kp), lambda i, j: (i, 0)), pl.BlockSpec((kp, tn), lambda i, j: (0, j))],
        out_specs=pl.BlockSpec((tm, tn), lambda i, j: (i, j)),
        compiler_params=_params(("parallel", "arbitrary")),
        name="matmul_ple",
    )(xn, w_gate, res, p, w_up)


def _hgrn2_level_table(c):
    t = np.arange(c)[:, None]
    s = np.arange(c)[None, :]
    x = np.bitwise_xor(t, s)
    lvl = np.where(x > 0, np.floor(np.log2(np.maximum(x, 1))).astype(np.int32), -1)
    return np.where(s < t, lvl, -1).astype(np.int32)


def _hgrn2_kernel(q_ref, f_ref, v_ref, g_ref, lb_ref, gn_ref, lvl_ref, tri_ref, *rest, chunk, n_chunks, has_s0):
    if has_s0:
        s0_ref, o_ref, s_ref, st_ref, b_ref = rest
    else:
        o_ref, s_ref, st_ref, b_ref = rest
    tb = pl.program_id(2)
    c = chunk
    dot = functools.partial(jnp.dot, preferred_element_type=F32)
    dot_nt = lambda a, b: lax.dot_general(a, b, (((1,), (1,)), ((), ())), preferred_element_type=F32)

    @pl.when(tb == 0)
    def _():
        if has_s0:
            st_ref[...] = s0_ref[0, 0].T
        else:
            st_ref[...] = jnp.zeros_like(st_ref)

    lb = lb_ref[...]
    gn = gn_ref[...]
    lvl = lvl_ref[...]
    tri = tri_ref[...]
    row = lax.broadcasted_iota(jnp.int32, (c, HEAD), 0)
    sub = lax.broadcasted_iota(jnp.int32, (8, HEAD), 0)
    n_levels = int(math.log2(c))

    def one_chunk(ci, carry):
        r0 = pl.multiple_of(ci * c, c)
        rows = pl.ds(r0, c)
        q = jax.nn.silu(q_ref[rows, :])
        f = lb + (1.0 - lb) * jax.nn.sigmoid(f_ref[rows, :])
        k = 1.0 - f
        v = v_ref[rows, :]
        lg = jnp.log(f)
        l0, l1, l2 = _split3(lg)
        cs = dot(tri, jnp.concatenate([l0, l1, l2], axis=1))
        b = cs[:, :HEAD] + (cs[:, HEAD:2 * HEAD] + cs[:, 2 * HEAD:])
        b_ref[...] = b

        att = jnp.zeros((c, c), F32)
        for j in range(n_levels):
            m = 1 << j
            right = (row & m) != 0
            if m == 1:
                e = jnp.where(right, f, 1.0)
            else:
                pieces = []
                for g8 in range(c // 8):
                    if m >= 4:
                        r = (g8 * 8 // (2 * m)) * 2 * m + m - 1
                        pieces.append(jnp.broadcast_to(b_ref[r:r + 1, :], (8, HEAD)))
                    else:
                        top = jnp.broadcast_to(b_ref[g8 * 8 + 1:g8 * 8 + 2, :], (8, HEAD))
                        bot = jnp.broadcast_to(b_ref[g8 * 8 + 5:g8 * 8 + 6, :], (8, HEAD))
                        pieces.append(jnp.where(sub < 4, top, bot))
                ref_b = jnp.concatenate(pieces, axis=0)
                e = jnp.exp(jnp.where(right, b - ref_b, ref_b - b))
            sj = dot_nt((q * e).astype(BF16), (k * e).astype(BF16))
            att = jnp.where(lvl == j, sj, att)

        st = st_ref[...]
        b_end = b_ref[c - 1:c, :]
        o = dot(att.astype(BF16), v.astype(BF16))
        o = o + dot_nt((q * jnp.exp(b)).astype(BF16), st.astype(BF16))
        o = o + jnp.sum(q * k, axis=-1, keepdims=True) * v
        kd = (k * jnp.exp(b_end - b)).astype(BF16)
        st_ref[...] = jnp.exp(b_end) * st + dot(v.T.astype(BF16), kd)

        o = o * lax.rsqrt(jnp.mean(o * o, axis=-1, keepdims=True) + EPS) * gn
        o_ref[rows, :] = (o * jax.nn.silu(g_ref[rows, :])).astype(o_ref.dtype)
        return carry

    lax.fori_loop(0, n_chunks, one_chunk, 0)

    @pl.when(tb == pl.num_programs(2) - 1)
    def _():
        s_ref[0, 0] = st_ref[...].T


def _hgrn2(proj, row_off, bsz, t, lb, gnorm, s0):
    d = proj.shape[1] // 4
    nh = d // HEAD
    c = min(t, 128)
    tb = _tile(t, 512)
    n_tb = t // tb
    assert row_off % tb == 0 and tb % c == 0
    blk0 = row_off // tb
    col = lambda kind: pl.BlockSpec((tb, HEAD), lambda b, h, i: (blk0 + b * n_tb + i, kind * nh + h))
    vec = pl.BlockSpec((1, HEAD), lambda b, h, i: (0, h))
    const = lambda shape: pl.BlockSpec(shape, lambda b, h, i: (0, 0))
    st_spec = pl.BlockSpec((1, 1, HEAD, HEAD), lambda b, h, i: (b, h, 0, 0))
    in_specs = [col(0), col(1), col(2), col(3), vec, const((1, HEAD)), const((c, c)), const((c, c))]
    args = [proj, proj, proj, proj, lb.reshape(1, d), gnorm.reshape(1, HEAD),
            jnp.asarray(_hgrn2_level_table(c)), jnp.asarray(np.tril(np.ones((c, c), np.float32)), BF16)]
    if s0 is not None:
        in_specs.append(st_spec)
        args.append(s0)
    return pl.pallas_call(
        functools.partial(_hgrn2_kernel, chunk=c, n_chunks=tb // c, has_s0=s0 is not None),
        out_shape=(jax.ShapeDtypeStruct((bsz * t, d), BF16), jax.ShapeDtypeStruct((bsz, nh, HEAD, HEAD), F32)),
        grid=(bsz, nh, n_tb),
        in_specs=in_specs,
        out_specs=(pl.BlockSpec((tb, HEAD), lambda b, h, i: (b * n_tb + i, h)), st_spec),
        scratch_shapes=[pltpu.VMEM((HEAD, HEAD), F32), pltpu.VMEM((c, HEAD), F32)],
        compiler_params=_params(("parallel", "parallel", "arbitrary")),
        name="hgrn2_scan",
    )(*args)


def _softmax_rows(s):
    m = jnp.max(s, axis=-1, keepdims=True)
    e = jnp.exp(s - m)
    return e / jnp.sum(e, axis=-1, keepdims=True)


def _subln(o, g, post_scale):
    return o * lax.rsqrt(jnp.mean(o * o, axis=-1, keepdims=True) + EPS) * g * post_scale


def _attn_prompt_kernel(lam_ref, q_ref, k_ref, v_ref, g_ref, o_ref, kb_ref, vb_ref, *, tq, ext_step, extents,
                        post_scale):
    qi = pl.program_id(2)
    dot_nt = lambda a, b: lax.dot_general(a, b, (((1,), (1,)), ((), ())), preferred_element_type=F32)

    @pl.when(qi == 0)
    def _():
        kb_ref[...] = k_ref[...].astype(BF16)
        vb_ref[...] = v_ref[...].astype(BF16)

    lam = lam_ref[0]
    q = q_ref[...]
    per = ext_step // tq
    for vi, ext in enumerate(extents):
        @pl.when(qi // per == vi)
        def _(ext=ext):
            pos = qi * tq + lax.broadcasted_iota(jnp.int32, (tq, ext), 0)
            limit = (pos | (STREAM_CHUNK - 1)) + 1
            allowed = lax.broadcasted_iota(jnp.int32, (tq, ext), 1) < limit
            s1 = jnp.where(allowed, dot_nt(q[:, :HEAD], kb_ref[0:ext, 0:HEAD]), -jnp.inf)
            s2 = jnp.where(allowed, dot_nt(q[:, HEAD:], kb_ref[0:ext, HEAD:2 * HEAD]), -jnp.inf)
            w = _softmax_rows(s1) - lam * _softmax_rows(s2)
            o = jnp.dot(w.astype(BF16), vb_ref[0:ext, :], preferred_element_type=F32)
            o_ref[...] = _subln(o, g_ref[...], post_scale).astype(o_ref.dtype)


def _attn_prompt(q, k, v, lam, subln, post_scale, bsz, t):
    d = q.shape[1]
    hw = 2 * HEAD
    nh = d // hw
    tq = _tile(t, 128)
    n_q = t // tq
    n_var = _tile(n_q, 4)
    ext_step = (n_q // n_var) * tq
    extents = tuple((vi + 1) * ext_step for vi in range(n_var))
    return pl.pallas_call(
        functools.partial(_attn_prompt_kernel, tq=tq, ext_step=ext_step, extents=extents, post_scale=post_scale),
        out_shape=jax.ShapeDtypeStruct((bsz * t, d), BF16),
        grid=(bsz, nh, n_q),
        in_specs=[pl.BlockSpec(memory_space=pltpu.SMEM),
                  pl.BlockSpec((tq, hw), lambda b, h, i: (b * n_q + i, h)),
                  pl.BlockSpec((t, hw), lambda b, h, i: (b, h)),
                  pl.BlockSpec((t, hw), lambda b, h, i: (b, h)),
                  pl.BlockSpec((1, hw), lambda b, h, i: (0, 0))],
        out_specs=pl.BlockSpec((tq, hw), lambda b, h, i: (b * n_q + i, h)),
        scratch_shapes=[pltpu.VMEM((t, hw), BF16), pltpu.VMEM((t, hw), BF16)],
        compiler_params=_params(("parallel", "parallel", "arbitrary")),
        name="diff_attn_prompt",
    )(lam.reshape(1), q, k, v, subln.reshape(1, hw))


def _attn_sample_kernel(lam_ref, q_ref, kn_ref, vn_ref, ck_ref, cv_ref, g_ref, o_ref, *, post_scale):
    dot_nt = lambda a, b: lax.dot_general(a, b, (((1,), (1,)), ((), ())), preferred_element_type=F32)
    lam = lam_ref[0]
    q = q_ref[...]
    ck = ck_ref[...].astype(BF16)
    kn = kn_ref[...].astype(BF16)

    def probs(c0):
        qc = q[:, c0:c0 + HEAD]
        sp = dot_nt(qc, ck[:, c0:c0 + HEAD])
        sn = dot_nt(qc, kn[:, c0:c0 + HEAD])
        m = jnp.maximum(jnp.max(sp, axis=-1, keepdims=True), jnp.max(sn, axis=-1, keepdims=True))
        ep, en = jnp.exp(sp - m), jnp.exp(sn - m)
        tot = jnp.sum(ep, axis=-1, keepdims=True) + jnp.sum(en, axis=-1, keepdims=True)
        return ep / tot, en / tot

    p1p, p1n = probs(0)
    p2p, p2n = probs(HEAD)
    o = jnp.dot((p1p - lam * p2p).astype(BF16), cv_ref[...].astype(BF16), preferred_element_type=F32)
    o = o + jnp.dot((p1n - lam * p2n).astype(BF16), vn_ref[...].astype(BF16), preferred_element_type=F32)
    o_ref[...] = _subln(o, g_ref[...], post_scale).astype(o_ref.dtype)


def _attn_sample(q, k, v, cache_k, cache_v, lam, subln, post_scale, row_off, bsz, t):
    d = q.shape[1]
    hw = 2 * HEAD
    nh = d // hw
    past = cache_k.shape[1]
    assert row_off % t == 0
    blk0 = row_off // t
    new = pl.BlockSpec((t, hw), lambda b, h: (blk0 + b, h))
    old = pl.BlockSpec((past, hw), lambda b, h: (b, h))
    return pl.pallas_call(
        functools.partial(_attn_sample_kernel, post_scale=post_scale),
        out_shape=jax.ShapeDtypeStruct((bsz * t, d), BF16),
        grid=(bsz, nh),
        in_specs=[pl.BlockSpec(memory_space=pltpu.SMEM), new, new, new, old, old,
                  pl.BlockSpec((1, hw), lambda b, h: (0, 0))],
        out_specs=pl.BlockSpec((t, hw), lambda b, h: (b, h)),
        compiler_params=_params(("parallel", "parallel")),
        name="diff_attn_sample",
    )(lam.reshape(1), q, k, v, cache_k.reshape(bsz * past, d), cache_v.reshape(bsz * past, d),
      subln.reshape(1, hw))


def _moe_up_kernel(be_ref, x_ref, wg_ref, wu_ref, o_ref):
    x = x_ref[...]
    g = jnp.dot(x, wg_ref[...], preferred_element_type=F32)
    u = jnp.dot(x, wu_ref[...], preferred_element_type=F32)
    o_ref[...] = (jax.nn.silu(g) * u).astype(o_ref.dtype)


def _moe_down_kernel(be_ref, h_ref, wd_ref, o_ref):
    o_ref[...] = jnp.dot(h_ref[...], wd_ref[...], preferred_element_type=F32)


def _moe_experts(xs, blk_exp, w_gate, w_up, w_down):
    p, d = xs.shape
    de = w_gate.shape[2]
    nb = p // MOE_ROWS
    te = _tile(de, 512)
    hidden = pl.pallas_call(
        _moe_up_kernel,
        out_shape=jax.ShapeDtypeStruct((p, de), BF16),
        grid_spec=pltpu.PrefetchScalarGridSpec(
            num_scalar_prefetch=1, grid=(de // te, nb),
            in_specs=[pl.BlockSpec((MOE_ROWS, d), lambda j, b, be: (b, 0)),
                      pl.BlockSpec((None, d, te), lambda j, b, be: (be[b], 0, j)),
                      pl.BlockSpec((None, d, te), lambda j, b, be: (be[b], 0, j))],
            out_specs=pl.BlockSpec((MOE_ROWS, te), lambda j, b, be: (b, j))),
        compiler_params=_params(("parallel", "arbitrary")),
        name="moe_gate_up",
    )(blk_exp, xs, w_gate, w_up)
    tn = _tile(d, 2048)
    return pl.pallas_call(
        _moe_down_kernel,
        out_shape=jax.ShapeDtypeStruct((p, d), F32),
        grid_spec=pltpu.PrefetchScalarGridSpec(
            num_scalar_prefetch=1, grid=(d // tn, nb),
            in_specs=[pl.BlockSpec((MOE_ROWS, de), lambda j, b, be: (b, 0)),
                      pl.BlockSpec((None, de, tn), lambda j, b, be: (be[b], 0, j))],
            out_specs=pl.BlockSpec((MOE_ROWS, tn), lambda j, b, be: (b, j))),
        compiler_params=_params(("parallel", "arbitrary")),
        name="moe_down",
    )(blk_exp, hidden, w_down)


def _route(logits, bg, be):
    n = logits.shape[0]
    pg = jax.nn.softmax(logits[:, :N_GROUPS] + bg, axis=-1)
    pg_top, g_top = lax.top_k(pg, 1)
    el = (logits[:, N_GROUPS:N_GROUPS + N_EXPERTS] + be).reshape(n, N_GROUPS, EXPERTS_PER_GROUP)
    el_sel = jnp.take_along_axis(el, g_top[:, :, None], axis=1)[:, 0]
    pe_top, e_top = lax.top_k(jax.nn.softmax(el_sel, axis=-1), TOP_K)
    gates = pg_top * pe_top / jnp.sum(pe_top, axis=-1, keepdims=True)
    return g_top * EXPERTS_PER_GROUP + e_top, gates


def _dispatch(expert_idx):
    n = expert_idx.shape[0]
    a = n * TOP_K
    flat_e = expert_idx.reshape(-1).astype(jnp.int32)
    order = jnp.argsort(flat_e).astype(jnp.int32)
    se = flat_e[order]
    stok = order // TOP_K
    counts = jnp.bincount(flat_e, length=N_EXPERTS).astype(jnp.int32)
    padded = (counts + MOE_ROWS - 1) // MOE_ROWS * MOE_ROWS
    start = jnp.cumsum(counts) - counts
    pend = jnp.cumsum(padded)
    pstart = pend - padded
    n_blocks = -(-(a + N_EXPERTS * (MOE_ROWS - 1)) // MOE_ROWS)
    p = n_blocks * MOE_ROWS
    blk_start = jnp.arange(n_blocks, dtype=jnp.int32) * MOE_ROWS
    blk_exp = jnp.minimum(jnp.searchsorted(pend, blk_start, side='right'), N_EXPERTS - 1).astype(jnp.int32)
    slot = jnp.arange(p, dtype=jnp.int32)
    slot_e = jnp.repeat(blk_exp, MOE_ROWS)
    within = slot - pstart[slot_e]
    valid = within < counts[slot_e]
    slot_tok = jnp.where(valid, stok[jnp.clip(start[slot_e] + within, 0, a - 1)], 0)
    dest_sorted = pstart[se] + (jnp.arange(a, dtype=jnp.int32) - start[se])
    pos = jnp.zeros((a,), jnp.int32).at[order].set(dest_sorted).reshape(n, TOP_K)
    return slot_tok, blk_exp, pos


def kernel(x_prompt, x_sample, state_hgrn, cache_k, cache_v, p_prompt, p_sample, norm_mix, norm_ffn, norm_ple, norm_final, a_w_in, a_w_out, a_lb_logits, a_gnorm, b_w_in, b_w_out, b_lambda_q1, b_lambda_k1, b_lambda_q2, b_lambda_k2, b_subln, router_group_w, router_group_b, router_expert_w, router_expert_b, expert_w_gate, expert_w_up, expert_w_down, ple_w_up, ple_w_gate):
    bp, tp, d = x_prompt.shape
    bs, ts, _ = x_sample.shape
    n_p, n_s = bp * tp, bs * ts
    depth = norm_mix.shape[0]
    nh_a = d // HEAD
    nh_b = d // (2 * HEAD)

    h = jnp.concatenate([x_prompt.reshape(n_p, d), x_sample.reshape(n_s, d)], axis=0)
    lb_all = jnp.cumsum(jax.nn.softmax(a_lb_logits.astype(F32), axis=0), axis=0)
    states_p, states_s, ks, vs = [], [], [], []

    for i in range(depth):
        j = i // 2
        xn = _rmsnorm(h, norm_mix[i], BF16)
        if i % 2 == 0:
            proj = _matmul(xn, a_w_in[j].astype(BF16))
            o_p, s_p = _hgrn2(proj, 0, bp, tp, lb_all[j], a_gnorm[j], None)
            o_s, s_s = _hgrn2(proj, n_p, bs, ts, lb_all[j], a_gnorm[j], state_hgrn[j])
            states_p.append(s_p)
            states_s.append(s_s)
            w_out = a_w_out[j]
        else:
            w_in = b_w_in[j]
            q = _matmul(xn, w_in[:, :d].astype(BF16), out_dtype=BF16, scale=HEAD ** -0.5)
            k = _matmul(xn, w_in[:, d:2 * d].astype(BF16))
            v = _matmul(xn, w_in[:, 2 * d:].astype(BF16))
            lam0 = 0.8 - 0.6 * math.exp(-0.3 * i)
            lam = (jnp.exp(jnp.sum(b_lambda_q1[j] * b_lambda_k1[j]))
                   - jnp.exp(jnp.sum(b_lambda_q2[j] * b_lambda_k2[j])) + lam0).astype(F32)
            o_p = _attn_prompt(q, k, v, lam, b_subln[j], 1.0 - lam0, bp, tp)
            o_s = _attn_sample(q, k, v, cache_k[j], cache_v[j], lam, b_subln[j], 1.0 - lam0, n_p, bs, ts)
            ks.append(k)
            vs.append(v)
            w_out = b_w_out[j]
        h = _matmul_residual(jnp.concatenate([o_p, o_s], axis=0), w_out.astype(BF16), h)

        w_router = jnp.zeros((d, ROUTER_LANES), F32)
        w_router = w_router.at[:, :N_GROUPS].set(router_group_w[i])
        w_router = w_router.at[:, N_GROUPS:N_GROUPS + N_EXPERTS].set(router_expert_w[i])
        xn, logits = _rmsnorm_router(h, norm_ffn[i], w_router)
        expert_idx, gates = _route(logits, router_group_b[i], router_expert_b[i])
        slot_tok, blk_exp, pos = _dispatch(expert_idx)
        yb = _moe_experts(jnp.take(xn, slot_tok, axis=0), blk_exp, expert_w_gate[i].astype(BF16),
                          expert_w_up[i].astype(BF16), expert_w_down[i].astype(BF16))
        h = h + (jnp.take(yb, pos[:, 0], axis=0) * gates[:, 0:1] + jnp.take(yb, pos[:, 1], axis=0) * gates[:, 1:2])

        p_i = jnp.concatenate([p_prompt[i].reshape(n_p, -1), p_sample[i].reshape(n_s, -1)], axis=0)
        xn = _rmsnorm(h, norm_ple[i], BF16)
        h = _matmul_ple(xn, ple_w_gate[i].astype(BF16), h, p_i.astype(BF16), ple_w_up[i].astype(BF16))

    y = _rmsnorm(h, norm_final, F32)
    stack = lambda xs, lo, hi, shape: jnp.stack([x[lo:hi].reshape(shape) for x in xs])
    return (y[:n_p].reshape(bp, tp, d), y[n_p:].reshape(bs, ts, d),
            jnp.stack(states_p), jnp.stack(states_s),
            stack(ks, 0, n_p, (bp, tp, nh_b, 2, HEAD)), stack(vs, 0, n_p, (bp, tp, nh_b, 2 * HEAD)),
            stack(ks, n_p, n_p + n_s, (bs, ts, nh_b, 2, HEAD)), stack(vs, n_p, n_p + n_s, (bs, ts, nh_b, 2 * HEAD)))
```

```python
import functools
import math

import jax
import jax.numpy as jnp
import numpy as np
from jax import lax
from jax.experimental import pallas as pl
from jax.experimental.pallas import tpu as pltpu

F32 = jnp.float32
BF16 = jnp.bfloat16

EPS = 1e-6
HEAD = 128
STREAM_CHUNK = 64
N_GROUPS = 4
EXPERTS_PER_GROUP = 8
N_EXPERTS = N_GROUPS * EXPERTS_PER_GROUP
TOP_K = 2
ROUTER_LANES = 128
MOE_ROWS = 256
HGRN2_HEADS_PER_STEP = 4
V7X_VMEM_LIMIT = 56 * 1024 * 1024
LOG2E = math.log2(math.e)


def _tile(dim, want):
    t = min(dim, want)
    while dim % t:
        t //= 2
    return t


def _params(sem):
    return pltpu.CompilerParams(dimension_semantics=sem, vmem_limit_bytes=V7X_VMEM_LIMIT)


def _split3(x):
    hi = x.astype(BF16)
    r1 = x - hi.astype(F32)
    mid = r1.astype(BF16)
    lo = (r1 - mid.astype(F32)).astype(BF16)
    return hi, mid, lo


def _dot(a, b):
    return jnp.dot(a, b, preferred_element_type=F32)


def _dot_nt(a, b):
    return lax.dot_general(a, b, (((1,), (1,)), ((), ())), preferred_element_type=F32)


def _rmsnorm_kernel(x_ref, g_ref, o_ref):
    x = x_ref[...]
    y = x * lax.rsqrt(jnp.mean(x * x, axis=-1, keepdims=True) + EPS)
    o_ref[...] = (y * g_ref[...]).astype(o_ref.dtype)


def _rmsnorm(x, g, out_dtype, row0=0, rows=None):
    n, d = x.shape
    rows = n if rows is None else rows
    tm = _tile(math.gcd(rows, row0) if row0 else rows, 512)
    blk0 = row0 // tm
    return pl.pallas_call(
        _rmsnorm_kernel,
        out_shape=jax.ShapeDtypeStruct((rows, d), out_dtype),
        grid=(rows // tm,),
        in_specs=[pl.BlockSpec((tm, d), lambda i: (blk0 + i, 0)), pl.BlockSpec((1, d), lambda i: (0, 0))],
        out_specs=pl.BlockSpec((tm, d), lambda i: (i, 0)),
        compiler_params=_params(("parallel",)),
        name="rmsnorm",
    )(x, g.reshape(1, d))


def _rmsnorm_router_kernel(x_ref, g_ref, w0_ref, w1_ref, w2_ref, o_ref, l_ref):
    x = x_ref[...]
    y = x * lax.rsqrt(jnp.mean(x * x, axis=-1, keepdims=True) + EPS)
    xn = y * g_ref[...]
    o_ref[...] = xn.astype(o_ref.dtype)
    x0, x1, x2 = _split3(xn)
    w0, w1, w2 = w0_ref[...], w1_ref[...], w2_ref[...]
    small = _dot(x0, w2) + _dot(x1, w1) + _dot(x2, w0)
    mid = _dot(x0, w1) + _dot(x1, w0)
    l_ref[...] = _dot(x0, w0) + (mid + small)


def _rmsnorm_router(x, g, w_router):
    n, d = x.shape
    tm = _tile(n, 512)
    w0, w1, w2 = _split3(w_router)
    wspec = pl.BlockSpec((d, ROUTER_LANES), lambda i: (0, 0))
    return pl.pallas_call(
        _rmsnorm_router_kernel,
        out_shape=(jax.ShapeDtypeStruct((n, d), BF16), jax.ShapeDtypeStruct((n, ROUTER_LANES), F32)),
        grid=(n // tm,),
        in_specs=[pl.BlockSpec((tm, d), lambda i: (i, 0)), pl.BlockSpec((1, d), lambda i: (0, 0)),
                  wspec, wspec, wspec],
        out_specs=(pl.BlockSpec((tm, d), lambda i: (i, 0)), pl.BlockSpec((tm, ROUTER_LANES), lambda i: (i, 0))),
        compiler_params=_params(("parallel",)),
        name="rmsnorm_router",
    )(x, g.reshape(1, d), w0, w1, w2)


def _mm_kernel(x_ref, w_ref, o_ref, *, scale):
    acc = _dot(x_ref[...], w_ref[...])
    if scale is not None:
        acc = acc * scale
    o_ref[...] = acc.astype(o_ref.dtype)


def _mm_res2_kernel(xa_ref, xb_ref, w_ref, r_ref, o_ref, *, a_tiles):
    i = pl.program_id(0)

    @pl.when(i < a_tiles)
    def _():
        o_ref[...] = r_ref[...] + _dot(xa_ref[...], w_ref[...])

    @pl.when(i >= a_tiles)
    def _():
        o_ref[...] = r_ref[...] + _dot(xb_ref[...], w_ref[...])


def _mm_ple_kernel(x_ref, w_ref, r_ref, p_ref, wup_ref, o_ref):
    gate = jax.nn.sigmoid(_dot(x_ref[...], w_ref[...]))
    o_ref[...] = r_ref[...] + _dot(p_ref[...], wup_ref[...]) * gate


def _matmul(x, w, *, out_dtype=F32, scale=None, row0=0, rows=None, tm=1024, tn=1024):
    m, k = x.shape
    rows = m if rows is None else rows
    n = w.shape[1]
    tm = _tile(math.gcd(rows, row0) if row0 else rows, tm)
    tn = _tile(n, tn)
    blk0 = row0 // tm
    return pl.pallas_call(
        functools.partial(_mm_kernel, scale=scale),
        out_shape=jax.ShapeDtypeStruct((rows, n), out_dtype),
        grid=(rows // tm, n // tn),
        in_specs=[pl.BlockSpec((tm, k), lambda i, j: (blk0 + i, 0)), pl.BlockSpec((k, tn), lambda i, j: (0, j))],
        out_specs=pl.BlockSpec((tm, tn), lambda i, j: (i, j)),
        compiler_params=_params(("parallel", "arbitrary")),
        name="matmul",
    )(x, w)


def _matmul_residual2(xa, xb, w, res, *, tm=1024, tn=512):
    ma, k = xa.shape
    mb = xb.shape[0]
    n = w.shape[1]
    tm, tn = _tile(math.gcd(ma, mb), tm), _tile(n, tn)
    a_tiles, b_tiles = ma // tm, mb // tm
    return pl.pallas_call(
        functools.partial(_mm_res2_kernel, a_tiles=a_tiles),
        out_shape=jax.ShapeDtypeStruct((ma + mb, n), F32),
        grid=(a_tiles + b_tiles, n // tn),
        in_specs=[pl.BlockSpec((tm, k), lambda i, j: (jnp.minimum(i, a_tiles - 1), 0)),
                  pl.BlockSpec((tm, k), lambda i, j: (jnp.maximum(i - a_tiles, 0), 0)),
                  pl.BlockSpec((k, tn), lambda i, j: (0, j)),
                  pl.BlockSpec((tm, tn), lambda i, j: (i, j))],
        out_specs=pl.BlockSpec((tm, tn), lambda i, j: (i, j)),
        compiler_params=_params(("parallel", "arbitrary")),
        name="matmul_residual",
    )(xa, xb, w, res)


def _matmul_ple(xn, w_gate, res, p, w_up, *, tm=1024, tn=512):
    m, k = xn.shape
    n = w_gate.shape[1]
    kp = p.shape[1]
    tm, tn = _tile(m, tm), _tile(n, tn)
    return pl.pallas_call(
        _mm_ple_kernel,
        out_shape=jax.ShapeDtypeStruct((m, n), F32),
        grid=(m // tm, n // tn),
        in_specs=[pl.BlockSpec((tm, k), lambda i, j: (i, 0)), pl.BlockSpec((k, tn), lambda i, j: (0, j)),
                  pl.BlockSpec((tm, tn), lambda i, j: (i, j)),
                  pl.BlockSpec((tm, kp), lambda i, j: (i, 0)), pl.BlockSpec((kp, tn), lambda i, j: (0, j))],
        out_specs=pl.BlockSpec((tm, tn), lambda i, j: (i, j)),
        compiler_params=_params(("parallel", "arbitrary")),
        name="matmul_ple",
    )(xn, w_gate, res, p, w_up)


def _hgrn2_level_table(c):
    t = np.arange(c)[:, None]
    s = np.arange(c)[None, :]
    x = np.bitwise_xor(t, s)
    lvl = np.where(x > 0, np.floor(np.log2(np.maximum(x, 1))).astype(np.int32), -1)
    return np.where(s < t, lvl, -1).astype(np.int32)


def _hgrn2_kernel(q_ref, f_ref, v_ref, g_ref, lb_ref, gn_ref, lvl_ref, tri_ref, *rest, chunk, n_chunks, hp, has_s0):
    if has_s0:
        s0_ref, o_ref, s_ref, st_ref, b_ref = rest
    else:
        o_ref, s_ref, st_ref, b_ref = rest
    tb = pl.program_id(2)
    c = chunk

    @pl.when(tb == 0)
    def _():
        for hh in range(hp):
            st_ref[hh] = s0_ref[0, hh].T if has_s0 else jnp.zeros((HEAD, HEAD), F32)

    gn = gn_ref[...]
    lvl = lvl_ref[...]
    tri = tri_ref[...]
    row = lax.broadcasted_iota(jnp.int32, (c, HEAD), 0)
    sub = lax.broadcasted_iota(jnp.int32, (8, HEAD), 0)
    n_levels = int(math.log2(c))

    def one_head(rows, hh):
        lanes = slice(hh * HEAD, (hh + 1) * HEAD)
        lb = lb_ref[:, lanes]
        q = jax.nn.silu(q_ref[rows, lanes])
        f = lb + (1.0 - lb) * jax.nn.sigmoid(f_ref[rows, lanes])
        k = 1.0 - f
        v = v_ref[rows, lanes]
        lg = jnp.log(f)
        l0, l1, l2 = _split3(lg)
        cs = _dot(tri, jnp.concatenate([l0, l1, l2], axis=1))
        b = cs[:, :HEAD] + (cs[:, HEAD:2 * HEAD] + cs[:, 2 * HEAD:])
        b_ref[hh] = b

        att = jnp.zeros((c, c), F32)
        for j in range(n_levels):
            m = 1 << j
            right = (row & m) != 0
            if m == 1:
                e = jnp.where(right, f, 1.0)
            else:
                pieces = []
                for g8 in range(c // 8):
                    if m >= 4:
                        r = (g8 * 8 // (2 * m)) * 2 * m + m - 1
                        pieces.append(jnp.broadcast_to(b_ref[hh, r:r + 1, :], (8, HEAD)))
                    else:
                        top = jnp.broadcast_to(b_ref[hh, g8 * 8 + 1:g8 * 8 + 2, :], (8, HEAD))
                        bot = jnp.broadcast_to(b_ref[hh, g8 * 8 + 5:g8 * 8 + 6, :], (8, HEAD))
                        pieces.append(jnp.where(sub < 4, top, bot))
                ref_b = jnp.concatenate(pieces, axis=0)
                e = jnp.exp(jnp.where(right, b - ref_b, ref_b - b))
            sj = _dot_nt((q * e).astype(BF16), (k * e).astype(BF16))
            att = jnp.where(lvl == j, sj, att)

        st = st_ref[hh]
        b_end = b_ref[hh, c - 1:c, :]
        o = _dot(att.astype(BF16), v.astype(BF16))
        o = o + _dot_nt((q * jnp.exp(b)).astype(BF16), st.astype(BF16))
        o = o + jnp.sum(q * k, axis=-1, keepdims=True) * v
        kd = (k * jnp.exp(b_end - b)).astype(BF16)
        st_ref[hh] = jnp.exp(b_end) * st + _dot(v.T.astype(BF16), kd)

        o = o * lax.rsqrt(jnp.mean(o * o, axis=-1, keepdims=True) + EPS) * gn
        o_ref[rows, lanes] = (o * jax.nn.silu(g_ref[rows, lanes])).astype(o_ref.dtype)

    def one_chunk(ci, carry):
        rows = pl.ds(pl.multiple_of(ci * c, c), c)
        for hh in range(hp):
            one_head(rows, hh)
        return carry

    lax.fori_loop(0, n_chunks, one_chunk, 0)

    @pl.when(tb == pl.num_programs(2) - 1)
    def _():
        for hh in range(hp):
            s_ref[0, hh] = st_ref[hh].T


def _hgrn2(proj, row_off, bsz, t, lb, gnorm, s0):
    d = proj.shape[1] // 4
    nh = d // HEAD
    hp = _tile(nh, HGRN2_HEADS_PER_STEP)
    c = min(t, 128)
    tb = _tile(t, 512)
    n_tb = t // tb
    assert row_off % tb == 0 and tb % c == 0
    blk0 = row_off // tb
    ng = nh // hp
    col = lambda kind: pl.BlockSpec((tb, hp * HEAD), lambda b, h, i: (blk0 + b * n_tb + i, kind * ng + h))
    const = lambda shape: pl.BlockSpec(shape, lambda b, h, i: (0, 0))
    st_spec = pl.BlockSpec((1, hp, HEAD, HEAD), lambda b, h, i: (b, h, 0, 0))
    in_specs = [col(0), col(1), col(2), col(3), pl.BlockSpec((1, hp * HEAD), lambda b, h, i: (0, h)),
                const((1, HEAD)), const((c, c)), const((c, c))]
    args = [proj, proj, proj, proj, lb.reshape(1, d), gnorm.reshape(1, HEAD),
            jnp.asarray(_hgrn2_level_table(c)), jnp.asarray(np.tril(np.ones((c, c), np.float32)), BF16)]
    if s0 is not None:
        in_specs.append(st_spec)
        args.append(s0)
    return pl.pallas_call(
        functools.partial(_hgrn2_kernel, chunk=c, n_chunks=tb // c, hp=hp, has_s0=s0 is not None),
        out_shape=(jax.ShapeDtypeStruct((bsz * t, d), BF16), jax.ShapeDtypeStruct((bsz, nh, HEAD, HEAD), F32)),
        grid=(bsz, ng, n_tb),
        in_specs=in_specs,
        out_specs=(pl.BlockSpec((tb, hp * HEAD), lambda b, h, i: (b * n_tb + i, h)), st_spec),
        scratch_shapes=[pltpu.VMEM((hp, HEAD, HEAD), F32), pltpu.VMEM((hp, c, HEAD), F32)],
        compiler_params=_params(("parallel", "parallel", "arbitrary")),
        name="hgrn2_scan",
    )(*args)


def _subln(o, g, post_scale):
    return o * lax.rsqrt(jnp.mean(o * o, axis=-1, keepdims=True) + EPS) * g * post_scale


def _attn_prompt_kernel(lam_ref, q_ref, k_ref, v_ref, g_ref, o_ref, kb_ref, vb_ref, *, tq, tail, extents, post_scale):
    qi = pl.program_id(2)

    @pl.when(qi == 0)
    def _():
        kb_ref[...] = k_ref[...].astype(BF16)
        vb_ref[...] = v_ref[...].astype(BF16)

    lam = lam_ref[0]
    q = q_ref[...]
    per = tail // tq
    for vi, ext in enumerate(extents):
        @pl.when(qi // per == vi)
        def _(ext=ext):
            head = ext - tail
            pos = qi * tq + lax.broadcasted_iota(jnp.int32, (tq, tail), 0)
            limit = (pos | (STREAM_CHUNK - 1)) + 1
            allowed = head + lax.broadcasted_iota(jnp.int32, (tq, tail), 1) < limit
            v_tail = vb_ref[head:ext, :]

            def one_map(c0):
                qc = q[:, c0:c0 + HEAD]
                st = jnp.where(allowed, _dot_nt(qc, kb_ref[head:ext, c0:c0 + HEAD]), -jnp.inf)
                m = jnp.max(st, axis=-1, keepdims=True)
                if head:
                    sh = _dot_nt(qc, kb_ref[0:head, c0:c0 + HEAD])
                    m = jnp.maximum(m, jnp.max(sh, axis=-1, keepdims=True))
                et = jnp.exp2(st - m)
                tot = jnp.sum(et, axis=-1, keepdims=True)
                acc = _dot(et.astype(BF16), v_tail)
                if head:
                    eh = jnp.exp2(sh - m)
                    tot = tot + jnp.sum(eh, axis=-1, keepdims=True)
                    acc = acc + _dot(eh.astype(BF16), vb_ref[0:head, :])
                return acc / tot

            o = one_map(0) - lam * one_map(HEAD)
            o_ref[...] = _subln(o, g_ref[...], post_scale).astype(o_ref.dtype)


def _attn_prompt(q, k, v, lam, subln, post_scale, bsz, t):
    d = q.shape[1]
    hw = 2 * HEAD
    nh = d // hw
    tq = _tile(t, 128)
    n_q = t // tq
    tail = _tile(t, 2 * tq)
    extents = tuple(range(tail, t + 1, tail))
    return pl.pallas_call(
        functools.partial(_attn_prompt_kernel, tq=tq, tail=tail, extents=extents, post_scale=post_scale),
        out_shape=jax.ShapeDtypeStruct((bsz * t, d), BF16),
        grid=(bsz, nh, n_q),
        in_specs=[pl.BlockSpec(memory_space=pltpu.SMEM),
                  pl.BlockSpec((tq, hw), lambda b, h, i: (b * n_q + i, h)),
                  pl.BlockSpec((t, hw), lambda b, h, i: (b, h)),
                  pl.BlockSpec((t, hw), lambda b, h, i: (b, h)),
                  pl.BlockSpec((1, hw), lambda b, h, i: (0, 0))],
        out_specs=pl.BlockSpec((tq, hw), lambda b, h, i: (b * n_q + i, h)),
        scratch_shapes=[pltpu.VMEM((t, hw), BF16), pltpu.VMEM((t, hw), BF16)],
        compiler_params=_params(("parallel", "parallel", "arbitrary")),
        name="diff_attn_prompt",
    )(lam.reshape(1), q, k, v, subln.reshape(1, hw))


def _attn_sample_kernel(lam_ref, q_ref, kn_ref, vn_ref, ck_ref, cv_ref, g_ref, o_ref, m_ref, l_ref, acc_ref, *,
                        sp, nh, post_scale):
    pb = pl.program_id(1)
    hw = 2 * HEAD

    @pl.when(pb == 0)
    def _():
        m_ref[...] = jnp.full(m_ref.shape, -jnp.inf, F32)
        l_ref[...] = jnp.zeros(l_ref.shape, F32)
        acc_ref[...] = jnp.zeros(acc_ref.shape, F32)

    def update(idx, s, v):
        m_old = m_ref[idx]
        m_new = jnp.maximum(m_old, jnp.max(s, axis=-1, keepdims=True))
        alpha = jnp.exp2(m_old - m_new)
        e = jnp.exp2(s - m_new)
        l_ref[idx] = alpha * l_ref[idx] + jnp.sum(e, axis=-1, keepdims=True)
        acc_ref[idx] = alpha * acc_ref[idx] + _dot(e.astype(BF16), v)
        m_ref[idx] = m_new

    for h in range(nh):
        v = cv_ref[:, h * hw:(h + 1) * hw].astype(BF16)
        for c in range(2):
            idx = 2 * h + c
            k = ck_ref[pl.ds(idx, sp, stride=2 * nh), :].astype(BF16)
            update(idx, _dot_nt(q_ref[:, idx * HEAD:(idx + 1) * HEAD], k), v)

    @pl.when(pb == pl.num_programs(1) - 1)
    def _():
        lam = lam_ref[0]
        for h in range(nh):
            v = vn_ref[:, h * hw:(h + 1) * hw].astype(BF16)
            outs = []
            for c in range(2):
                idx = 2 * h + c
                k = kn_ref[:, idx * HEAD:(idx + 1) * HEAD].astype(BF16)
                update(idx, _dot_nt(q_ref[:, idx * HEAD:(idx + 1) * HEAD], k), v)
                outs.append(acc_ref[idx] / l_ref[idx])
            o = outs[0] - lam * outs[1]
            o_ref[:, h * hw:(h + 1) * hw] = _subln(o, g_ref[...], post_scale).astype(o_ref.dtype)


def _attn_sample(q, q_row0, k, v, cache_k, cache_v, lam, subln, post_scale, bsz, t):
    d = q.shape[1]
    hw = 2 * HEAD
    nh = d // hw
    past = cache_k.shape[1]
    sp = _tile(past, 512)
    n_pb = past // sp
    assert q_row0 % t == 0
    blk0 = q_row0 // t
    new = pl.BlockSpec((t, d), lambda b, p: (b, 0))
    return pl.pallas_call(
        functools.partial(_attn_sample_kernel, sp=sp, nh=nh, post_scale=post_scale),
        out_shape=jax.ShapeDtypeStruct((bsz * t, d), BF16),
        grid=(bsz, n_pb),
        in_specs=[pl.BlockSpec(memory_space=pltpu.SMEM),
                  pl.BlockSpec((t, d), lambda b, p: (blk0 + b, 0)), new, new,
                  pl.BlockSpec((sp * 2 * nh, HEAD), lambda b, p: (b * n_pb + p, 0)),
                  pl.BlockSpec((sp, d), lambda b, p: (b * n_pb + p, 0)),
                  pl.BlockSpec((1, hw), lambda b, p: (0, 0))],
        out_specs=pl.BlockSpec((t, d), lambda b, p: (b, 0)),
        scratch_shapes=[pltpu.VMEM((2 * nh, t, 1), F32), pltpu.VMEM((2 * nh, t, 1), F32),
                        pltpu.VMEM((2 * nh, t, hw), F32)],
        compiler_params=_params(("parallel", "arbitrary")),
        name="diff_attn_sample",
    )(lam.reshape(1), q, k, v, cache_k.reshape(bsz * past * 2 * nh, HEAD), cache_v.reshape(bsz * past, d),
      subln.reshape(1, hw))


def _expert_changed(be_ref):
    b = pl.program_id(1)
    return (b == 0) | (be_ref[b] != be_ref[jnp.maximum(b - 1, 0)])


def _moe_up_kernel(be_ref, x_ref, wg_ref, wu_ref, o_ref, wgb_ref, wub_ref):
    @pl.when(_expert_changed(be_ref))
    def _():
        wgb_ref[...] = wg_ref[...].astype(BF16)
        wub_ref[...] = wu_ref[...].astype(BF16)

    x = x_ref[...]
    o_ref[...] = (jax.nn.silu(_dot(x, wgb_ref[...])) * _dot(x, wub_ref[...])).astype(o_ref.dtype)


def _moe_down_kernel(be_ref, h_ref, wd_ref, o_ref, wdb_ref):
    @pl.when(_expert_changed(be_ref))
    def _():
        wdb_ref[...] = wd_ref[...].astype(BF16)

    o_ref[...] = _dot(h_ref[...], wdb_ref[...])


def _moe_experts(xs, blk_exp, layer, w_gate, w_up, w_down):
    p, d = xs.shape
    de = w_gate.shape[3]
    nb = p // MOE_ROWS
    te = _tile(de, 512)
    hidden = pl.pallas_call(
        _moe_up_kernel,
        out_shape=jax.ShapeDtypeStruct((p, de), BF16),
        grid_spec=pltpu.PrefetchScalarGridSpec(
            num_scalar_prefetch=1, grid=(de // te, nb),
            in_specs=[pl.BlockSpec((MOE_ROWS, d), lambda j, b, be: (b, 0)),
                      pl.BlockSpec((None, None, d, te), lambda j, b, be: (layer, be[b], 0, j)),
                      pl.BlockSpec((None, None, d, te), lambda j, b, be: (layer, be[b], 0, j))],
            out_specs=pl.BlockSpec((MOE_ROWS, te), lambda j, b, be: (b, j)),
            scratch_shapes=[pltpu.VMEM((d, te), BF16), pltpu.VMEM((d, te), BF16)]),
        compiler_params=_params(("arbitrary", "arbitrary")),
        name="moe_gate_up",
    )(blk_exp, xs, w_gate, w_up)
    tn = _tile(d, 2048)
    return pl.pallas_call(
        _moe_down_kernel,
        out_shape=jax.ShapeDtypeStruct((p, d), F32),
        grid_spec=pltpu.PrefetchScalarGridSpec(
            num_scalar_prefetch=1, grid=(d // tn, nb),
            in_specs=[pl.BlockSpec((MOE_ROWS, de), lambda j, b, be: (b, 0)),
                      pl.BlockSpec((None, None, de, tn), lambda j, b, be: (layer, be[b], 0, j))],
            out_specs=pl.BlockSpec((MOE_ROWS, tn), lambda j, b, be: (b, j)),
            scratch_shapes=[pltpu.VMEM((de, tn), BF16)]),
        compiler_params=_params(("arbitrary", "arbitrary")),
        name="moe_down",
    )(blk_exp, hidden, w_down)


def _top1(x):
    n = x.shape[-1]
    m = jnp.max(x, axis=-1, keepdims=True)
    idx = jnp.min(jnp.where(x == m, jnp.arange(n, dtype=jnp.int32), n), axis=-1, keepdims=True)
    return m, idx


def _route(logits, bg, be):
    n = logits.shape[0]
    pg = jax.nn.softmax(logits[:, :N_GROUPS] + bg, axis=-1)
    pg_top, g_top = _top1(pg)
    el = (logits[:, N_GROUPS:N_GROUPS + N_EXPERTS] + be).reshape(n, N_GROUPS, EXPERTS_PER_GROUP)
    sel = g_top[:, :, None] == jnp.arange(N_GROUPS, dtype=jnp.int32)[None, :, None]
    el_sel = jnp.sum(jnp.where(sel, el, 0.0), axis=1)
    pe = jax.nn.softmax(el_sel, axis=-1)
    p1, e1 = _top1(pe)
    lane = jnp.arange(EXPERTS_PER_GROUP, dtype=jnp.int32)[None, :]
    p2, e2 = _top1(jnp.where(lane == e1, -jnp.inf, pe))
    pe_top = jnp.concatenate([p1, p2], axis=-1)
    e_top = jnp.concatenate([e1, e2], axis=-1)
    gates = pg_top * pe_top / jnp.sum(pe_top, axis=-1, keepdims=True)
    return g_top * EXPERTS_PER_GROUP + e_top, gates


def _dispatch(expert_idx):
    n = expert_idx.shape[0]
    a = n * TOP_K
    flat_e = expert_idx.reshape(-1).astype(jnp.int32)
    order = jnp.argsort(flat_e).astype(jnp.int32)
    se = flat_e[order]
    stok = order // TOP_K
    counts = jnp.bincount(flat_e, length=N_EXPERTS).astype(jnp.int32)
    padded = (counts + MOE_ROWS - 1) // MOE_ROWS * MOE_ROWS
    start = jnp.cumsum(counts) - counts
    pend = jnp.cumsum(padded)
    pstart = pend - padded
    n_blocks = -(-(a + N_EXPERTS * (MOE_ROWS - 1)) // MOE_ROWS)
    p = n_blocks * MOE_ROWS
    blk_start = jnp.arange(n_blocks, dtype=jnp.int32) * MOE_ROWS
    blk_exp = jnp.minimum(jnp.searchsorted(pend, blk_start, side='right'), N_EXPERTS - 1).astype(jnp.int32)
    slot = jnp.arange(p, dtype=jnp.int32)
    slot_e = jnp.repeat(blk_exp, MOE_ROWS)
    within = slot - pstart[slot_e]
    valid = within < counts[slot_e]
    slot_tok = jnp.where(valid, stok[jnp.clip(start[slot_e] + within, 0, a - 1)], 0)
    dest_sorted = pstart[se] + (jnp.arange(a, dtype=jnp.int32) - start[se])
    pos = jnp.zeros((a,), jnp.int32).at[order].set(dest_sorted).reshape(n, TOP_K)
    return slot_tok, blk_exp, pos


def _rows(x, idx):
    return x.at[idx].get(mode="promise_in_bounds")


def kernel(x_prompt, x_sample, state_hgrn, cache_k, cache_v, p_prompt, p_sample, norm_mix, norm_ffn, norm_ple, norm_final, a_w_in, a_w_out, a_lb_logits, a_gnorm, b_w_in, b_w_out, b_lambda_q1, b_lambda_k1, b_lambda_q2, b_lambda_k2, b_subln, router_group_w, router_group_b, router_expert_w, router_expert_b, expert_w_gate, expert_w_up, expert_w_down, ple_w_up, ple_w_gate):
    bp, tp, d = x_prompt.shape
    bs, ts, _ = x_sample.shape
    n_p, n_s = bp * tp, bs * ts
    depth = norm_mix.shape[0]
    nh_b = d // (2 * HEAD)

    h = jnp.concatenate([x_prompt.reshape(n_p, d), x_sample.reshape(n_s, d)], axis=0)
    lb_all = jnp.cumsum(jax.nn.softmax(a_lb_logits.astype(F32), axis=0), axis=0)
    states_p, states_s, k_p, v_p, k_s, v_s = [], [], [], [], [], []

    for i in range(depth):
        j = i // 2
        xn = _rmsnorm(h, norm_mix[i], BF16)
        if i % 2 == 0:
            proj = _matmul(xn, a_w_in[j].astype(BF16))
            o_p, s_p = _hgrn2(proj, 0, bp, tp, lb_all[j], a_gnorm[j], None)
            o_s, s_s = _hgrn2(proj, n_p, bs, ts, lb_all[j], a_gnorm[j], state_hgrn[j])
            states_p.append(s_p)
            states_s.append(s_s)
            w_out = a_w_out[j]
        else:
            w_in = b_w_in[j]
            wk, wv = w_in[:, d:2 * d].astype(BF16), w_in[:, 2 * d:].astype(BF16)
            q = _matmul(xn, w_in[:, :d].astype(BF16), out_dtype=BF16, scale=HEAD ** -0.5 * LOG2E)
            kp, vp = _matmul(xn, wk, rows=n_p), _matmul(xn, wv, rows=n_p)
            ks, vs = _matmul(xn, wk, row0=n_p, rows=n_s), _matmul(xn, wv, row0=n_p, rows=n_s)
            lam0 = 0.8 - 0.6 * math.exp(-0.3 * i)
            lam = (jnp.exp(jnp.sum(b_lambda_q1[j] * b_lambda_k1[j]))
                   - jnp.exp(jnp.sum(b_lambda_q2[j] * b_lambda_k2[j])) + lam0).astype(F32)
            o_p = _attn_prompt(q, kp, vp, lam, b_subln[j], 1.0 - lam0, bp, tp)
            o_s = _attn_sample(q, n_p, ks, vs, cache_k[j], cache_v[j], lam, b_subln[j], 1.0 - lam0, bs, ts)
            k_p.append(kp.reshape(bp, tp, nh_b, 2, HEAD))
            v_p.append(vp.reshape(bp, tp, nh_b, 2 * HEAD))
            k_s.append(ks.reshape(bs, ts, nh_b, 2, HEAD))
            v_s.append(vs.reshape(bs, ts, nh_b, 2 * HEAD))
            w_out = b_w_out[j]
        h = _matmul_residual2(o_p, o_s, w_out.astype(BF16), h)

        w_router = jnp.zeros((d, ROUTER_LANES), F32)
        w_router = w_router.at[:, :N_GROUPS].set(router_group_w[i])
        w_router = w_router.at[:, N_GROUPS:N_GROUPS + N_EXPERTS].set(router_expert_w[i])
        xn, logits = _rmsnorm_router(h, norm_ffn[i], w_router)
        expert_idx, gates = _route(logits, router_group_b[i], router_expert_b[i])
        slot_tok, blk_exp, pos = _dispatch(expert_idx)
        yb = _moe_experts(_rows(xn, slot_tok), blk_exp, i, expert_w_gate, expert_w_up, expert_w_down)
        h = h + (_rows(yb, pos[:, 0]) * gates[:, 0:1] + _rows(yb, pos[:, 1]) * gates[:, 1:2])

        p_i = jnp.concatenate([p_prompt[i].reshape(n_p, -1), p_sample[i].reshape(n_s, -1)], axis=0)
        xn = _rmsnorm(h, norm_ple[i], BF16)
        h = _matmul_ple(xn, ple_w_gate[i].astype(BF16), h, p_i.astype(BF16), ple_w_up[i].astype(BF16))

    y_p = _rmsnorm(h, norm_final, F32, 0, n_p)
    y_s = _rmsnorm(h, norm_final, F32, n_p, n_s)
    return (y_p.reshape(bp, tp, d), y_s.reshape(bs, ts, d), jnp.stack(states_p), jnp.stack(states_s),
            jnp.stack(k_p), jnp.stack(v_p), jnp.stack(k_s), jnp.stack(v_s))
```

```python
import functools
import math

import jax
import jax.numpy as jnp
import numpy as np
from jax import lax
from jax.experimental import pallas as pl
from jax.experimental.pallas import tpu as pltpu

F32 = jnp.float32
BF16 = jnp.bfloat16

EPS = 1e-6
HEAD = 128
STREAM_CHUNK = 64
N_GROUPS = 4
EXPERTS_PER_GROUP = 8
N_EXPERTS = N_GROUPS * EXPERTS_PER_GROUP
TOP_K = 2
ROUTER_LANES = 128
MOE_ROWS = 256
HGRN2_HEADS_PER_STEP = 8
ATTN_Q_ROWS = 512
V7X_VMEM_LIMIT = 56 * 1024 * 1024
LOG2E = math.log2(math.e)


def _tile(dim, want):
    t = min(dim, want)
    while dim % t:
        t //= 2
    return t


def _params(sem):
    return pltpu.CompilerParams(dimension_semantics=sem, vmem_limit_bytes=V7X_VMEM_LIMIT)


def _split3(x):
    hi = x.astype(BF16)
    r1 = x - hi.astype(F32)
    mid = r1.astype(BF16)
    lo = (r1 - mid.astype(F32)).astype(BF16)
    return hi, mid, lo


def _dot(a, b):
    return jnp.dot(a, b, preferred_element_type=F32)


def _dot_nt(a, b):
    return lax.dot_general(a, b, (((1,), (1,)), ((), ())), preferred_element_type=F32)


def _rmsnorm_kernel(x_ref, g_ref, o_ref):
    x = x_ref[...]
    y = x * lax.rsqrt(jnp.mean(x * x, axis=-1, keepdims=True) + EPS)
    o_ref[...] = (y * g_ref[...]).astype(o_ref.dtype)


def _rmsnorm(x, g, out_dtype, row0=0, rows=None):
    n, d = x.shape
    rows = n if rows is None else rows
    tm = _tile(math.gcd(rows, row0) if row0 else rows, 512)
    blk0 = row0 // tm
    return pl.pallas_call(
        _rmsnorm_kernel,
        out_shape=jax.ShapeDtypeStruct((rows, d), out_dtype),
        grid=(rows // tm,),
        in_specs=[pl.BlockSpec((tm, d), lambda i: (blk0 + i, 0)), pl.BlockSpec((1, d), lambda i: (0, 0))],
        out_specs=pl.BlockSpec((tm, d), lambda i: (i, 0)),
        compiler_params=_params(("parallel",)),
        name="rmsnorm",
    )(x, g.reshape(1, d))


def _rmsnorm_router_kernel(x_ref, g_ref, w0_ref, w1_ref, w2_ref, o_ref, l_ref):
    x = x_ref[...]
    y = x * lax.rsqrt(jnp.mean(x * x, axis=-1, keepdims=True) + EPS)
    xn = y * g_ref[...]
    o_ref[...] = _pack_halves(xn)
    x0, x1, x2 = _split3(xn)
    w0, w1, w2 = w0_ref[...], w1_ref[...], w2_ref[...]
    small = _dot(x0, w2) + _dot(x1, w1) + _dot(x2, w0)
    mid = _dot(x0, w1) + _dot(x1, w0)
    l_ref[...] = _dot(x0, w0) + (mid + small)


def _rmsnorm_router(x, g, w_router):
    n, d = x.shape
    tm = _tile(n, 512)
    w0, w1, w2 = _split3(w_router)
    wspec = pl.BlockSpec((d, ROUTER_LANES), lambda i: (0, 0))
    return pl.pallas_call(
        _rmsnorm_router_kernel,
        out_shape=(jax.ShapeDtypeStruct((n, d // 2), jnp.uint32), jax.ShapeDtypeStruct((n, ROUTER_LANES), F32)),
        grid=(n // tm,),
        in_specs=[pl.BlockSpec((tm, d), lambda i: (i, 0)), pl.BlockSpec((1, d), lambda i: (0, 0)),
                  wspec, wspec, wspec],
        out_specs=(pl.BlockSpec((tm, d // 2), lambda i: (i, 0)), pl.BlockSpec((tm, ROUTER_LANES), lambda i: (i, 0))),
        compiler_params=_params(("parallel",)),
        name="rmsnorm_router",
    )(x, g.reshape(1, d), w0, w1, w2)


def _mm_kernel(x_ref, w_ref, o_ref, *, scale):
    acc = _dot(x_ref[...], w_ref[...])
    if scale is not None:
        acc = acc * scale
    o_ref[...] = acc.astype(o_ref.dtype)


def _mm_rowsplit_kernel(x_ref, w_ref, o_ref, o16_ref):
    acc = _dot(x_ref[...], w_ref[...])
    o16_ref[...] = acc.astype(o16_ref.dtype)
    for c in range(o_ref.shape[1]):
        o_ref[:, c, :] = acc[:, c * HEAD:(c + 1) * HEAD]


def _matmul_rowsplit(x, w, *, row0=0, rows=None, tm=1024, tn=1024):
    m, k = x.shape
    rows = m if rows is None else rows
    n = w.shape[1]
    tm = _tile(math.gcd(rows, row0) if row0 else rows, tm)
    tn = _tile(n, tn)
    blk0 = row0 // tm
    return pl.pallas_call(
        _mm_rowsplit_kernel,
        out_shape=(jax.ShapeDtypeStruct((rows, n // HEAD, HEAD), F32), jax.ShapeDtypeStruct((rows, n), BF16)),
        grid=(rows // tm, n // tn),
        in_specs=[pl.BlockSpec((tm, k), lambda i, j: (blk0 + i, 0)), pl.BlockSpec((k, tn), lambda i, j: (0, j))],
        out_specs=(pl.BlockSpec((tm, tn // HEAD, HEAD), lambda i, j: (i, j, 0)),
                   pl.BlockSpec((tm, tn), lambda i, j: (i, j))),
        compiler_params=_params(("parallel", "arbitrary")),
        name="matmul_rowsplit",
    )(x, w)


def _mm_res2_kernel(xa_ref, xb_ref, w_ref, r_ref, o_ref, *, a_tiles):
    i = pl.program_id(0)

    @pl.when(i < a_tiles)
    def _():
        o_ref[...] = r_ref[...] + _dot(xa_ref[...], w_ref[...])

    @pl.when(i >= a_tiles)
    def _():
        o_ref[...] = r_ref[...] + _dot(xb_ref[...], w_ref[...])


def _mm_ple_kernel(x_ref, w_ref, r_ref, p_ref, wup_ref, o_ref):
    gate = jax.nn.sigmoid(_dot(x_ref[...], w_ref[...]))
    o_ref[...] = r_ref[...] + _dot(p_ref[...], wup_ref[...]) * gate


def _matmul(x, w, *, out_dtype=F32, scale=None, row0=0, rows=None, tm=1024, tn=1024):
    m, k = x.shape
    rows = m if rows is None else rows
    n = w.shape[1]
    tm = _tile(math.gcd(rows, row0) if row0 else rows, tm)
    tn = _tile(n, tn)
    blk0 = row0 // tm
    return pl.pallas_call(
        functools.partial(_mm_kernel, scale=scale),
        out_shape=jax.ShapeDtypeStruct((rows, n), out_dtype),
        grid=(rows // tm, n // tn),
        in_specs=[pl.BlockSpec((tm, k), lambda i, j: (blk0 + i, 0)), pl.BlockSpec((k, tn), lambda i, j: (0, j))],
        out_specs=pl.BlockSpec((tm, tn), lambda i, j: (i, j)),
        compiler_params=_params(("parallel", "arbitrary")),
        name="matmul",
    )(x, w)


def _matmul_residual2(xa, xb, w, res, *, tm=1024, tn=512):
    ma, k = xa.shape
    mb = xb.shape[0]
    n = w.shape[1]
    tm, tn = _tile(math.gcd(ma, mb), tm), _tile(n, tn)
    a_tiles, b_tiles = ma // tm, mb // tm
    return pl.pallas_call(
        functools.partial(_mm_res2_kernel, a_tiles=a_tiles),
        out_shape=jax.ShapeDtypeStruct((ma + mb, n), F32),
        grid=(a_tiles + b_tiles, n // tn),
        in_specs=[pl.BlockSpec((tm, k), lambda i, j: (jnp.minimum(i, a_tiles - 1), 0)),
                  pl.BlockSpec((tm, k), lambda i, j: (jnp.maximum(i - a_tiles, 0), 0)),
                  pl.BlockSpec((k, tn), lambda i, j: (0, j)),
                  pl.BlockSpec((tm, tn), lambda i, j: (i, j))],
        out_specs=pl.BlockSpec((tm, tn), lambda i, j: (i, j)),
        compiler_params=_params(("parallel", "arbitrary")),
        name="matmul_residual",
    )(xa, xb, w, res)


def _matmul_ple(xn, w_gate, res, p, w_up, *, tm=1024, tn=512):
    m, k = xn.shape
    n = w_gate.shape[1]
    kp = p.shape[1]
    tm, tn = _tile(m, tm), _tile(n, tn)
    return pl.pallas_call(
        _mm_ple_kernel,
        out_shape=jax.ShapeDtypeStruct((m, n), F32),
        grid=(m // tm, n // tn),
        in_specs=[pl.BlockSpec((tm, k), lambda i, j: (i, 0)), pl.BlockSpec((k, tn), lambda i, j: (0, j)),
                  pl.BlockSpec((tm, tn), lambda i, j: (i, j)),
                  pl.BlockSpec((tm, kp), lambda i, j: (i, 0)), pl.BlockSpec((kp, tn), lambda i, j: (0, j))],
        out_specs=pl.BlockSpec((tm, tn), lambda i, j: (i, j)),
        compiler_params=_params(("parallel", "arbitrary")),
        name="matmul_ple",
    )(xn, w_gate, res, p, w_up)


def _hgrn2_level_table(c):
    t = np.arange(c)[:, None]
    s = np.arange(c)[None, :]
    x = np.bitwise_xor(t, s)
    lvl = np.where(x > 0, np.floor(np.log2(np.maximum(x, 1))).astype(np.int32), -1)
    return np.where(s < t, lvl, -1).astype(np.int32)


def _hgrn2_sign_table(c):
    t = np.arange(c)[None, :, None]
    j = np.arange(int(math.log2(c)))[:, None, None]
    return np.broadcast_to(np.where((t >> j) & 1, 1.0, -1.0), (j.shape[0], c, HEAD)).astype(np.float32)


def _hgrn2_kernel(q_ref, f_ref, v_ref, g_ref, lb_ref, gn_ref, lvl_ref, tri_ref, sgn_ref, *rest, chunk, n_chunks, hp,
                  has_s0):
    if has_s0:
        s0_ref, o_ref, s_ref, st_ref, b_ref = rest
    else:
        o_ref, s_ref, st_ref, b_ref = rest
    tb = pl.program_id(2)
    c = chunk

    @pl.when(tb == 0)
    def _():
        for hh in range(hp):
            st_ref[hh] = s0_ref[0, hh].T if has_s0 else jnp.zeros((HEAD, HEAD), F32)

    gn = gn_ref[...]
    lvl = lvl_ref[...]
    tri = tri_ref[...]
    sub = lax.broadcasted_iota(jnp.int32, (8, HEAD), 0)
    n_levels = int(math.log2(c))

    def one_head(rows, hh):
        lanes = slice(hh * HEAD, (hh + 1) * HEAD)
        lb = lb_ref[:, lanes]
        q = jax.nn.silu(q_ref[rows, lanes])
        f = lb + (1.0 - lb) * jax.nn.sigmoid(f_ref[rows, lanes])
        k = 1.0 - f
        v = v_ref[rows, lanes]
        lg = jnp.log(f) * LOG2E
        l0, l1, l2 = _split3(lg)
        cs = _dot(tri, jnp.concatenate([l0, l1, l2], axis=1))
        b = cs[:, :HEAD] + (cs[:, HEAD:2 * HEAD] + cs[:, 2 * HEAD:])
        b_ref[hh] = b
        k16 = k.astype(BF16)

        att = jnp.where(lvl == 0, _dot_nt((q * f).astype(BF16), k16), 0.0)
        for j in range(1, n_levels):
            m = 1 << j
            pieces = []
            for g8 in range(c // 8):
                if m >= 4:
                    r = (g8 * 8 // (2 * m)) * 2 * m + m - 1
                    pieces.append(jnp.broadcast_to(b_ref[hh, r:r + 1, :], (8, HEAD)))
                else:
                    top = jnp.broadcast_to(b_ref[hh, g8 * 8 + 1:g8 * 8 + 2, :], (8, HEAD))
                    bot = jnp.broadcast_to(b_ref[hh, g8 * 8 + 5:g8 * 8 + 6, :], (8, HEAD))
                    pieces.append(jnp.where(sub < 4, top, bot))
            e = jnp.exp2((b - jnp.concatenate(pieces, axis=0)) * sgn_ref[j])
            sj = _dot_nt((q * e).astype(BF16), (k * e).astype(BF16))
            att = jnp.where(lvl == j, sj, att)

        st = st_ref[hh]
        b_end = b_ref[hh, c - 1:c, :]
        o = _dot(att.astype(BF16), v.astype(BF16))
        o = o + _dot_nt((q * jnp.exp2(b)).astype(BF16), st.astype(BF16))
        o = o + jnp.sum(q * k, axis=-1, keepdims=True) * v
        kd = (k * jnp.exp2(b_end - b)).astype(BF16)
        st_ref[hh] = jnp.exp2(b_end) * st + _dot(v.T.astype(BF16), kd)

        o = o * lax.rsqrt(jnp.mean(o * o, axis=-1, keepdims=True) + EPS) * gn
        o_ref[rows, lanes] = (o * jax.nn.silu(g_ref[rows, lanes])).astype(o_ref.dtype)

    def one_chunk(ci, carry):
        rows = pl.ds(pl.multiple_of(ci * c, c), c)
        for hh in range(hp):
            one_head(rows, hh)
        return carry

    lax.fori_loop(0, n_chunks, one_chunk, 0)

    @pl.when(tb == pl.num_programs(2) - 1)
    def _():
        for hh in range(hp):
            s_ref[0, hh] = st_ref[hh].T


def _hgrn2(proj, row_off, bsz, t, lb, gnorm, s0):
    d = proj.shape[1] // 4
    nh = d // HEAD
    hp = _tile(nh, HGRN2_HEADS_PER_STEP)
    c = min(t, 128)
    n_levels = int(math.log2(c))
    tb = _tile(t, 512)
    n_tb = t // tb
    assert row_off % tb == 0 and tb % c == 0
    blk0 = row_off // tb
    ng = nh // hp
    col = lambda kind: pl.BlockSpec((tb, hp * HEAD), lambda b, h, i: (blk0 + b * n_tb + i, kind * ng + h))
    const = lambda shape: pl.BlockSpec(shape, lambda b, h, i: (0, 0))
    st_spec = pl.BlockSpec((1, hp, HEAD, HEAD), lambda b, h, i: (b, h, 0, 0))
    in_specs = [col(0), col(1), col(2), col(3), pl.BlockSpec((1, hp * HEAD), lambda b, h, i: (0, h)),
                const((1, HEAD)), const((c, c)), const((c, c)),
                pl.BlockSpec((n_levels, c, HEAD), lambda b, h, i: (0, 0, 0))]
    args = [proj, proj, proj, proj, lb.reshape(1, d), gnorm.reshape(1, HEAD),
            jnp.asarray(_hgrn2_level_table(c)), jnp.asarray(np.tril(np.ones((c, c), np.float32)), BF16),
            jnp.asarray(_hgrn2_sign_table(c))]
    if s0 is not None:
        in_specs.append(st_spec)
        args.append(s0)
    return pl.pallas_call(
        functools.partial(_hgrn2_kernel, chunk=c, n_chunks=tb // c, hp=hp, has_s0=s0 is not None),
        out_shape=(jax.ShapeDtypeStruct((bsz * t, d), BF16), jax.ShapeDtypeStruct((bsz, nh, HEAD, HEAD), F32)),
        grid=(bsz, ng, n_tb),
        in_specs=in_specs,
        out_specs=(pl.BlockSpec((tb, hp * HEAD), lambda b, h, i: (b * n_tb + i, h)), st_spec),
        scratch_shapes=[pltpu.VMEM((hp, HEAD, HEAD), F32), pltpu.VMEM((hp, c, HEAD), F32)],
        compiler_params=_params(("parallel", "parallel", "arbitrary")),
        name="hgrn2_scan",
    )(*args)


def _subln(o, g, post_scale):
    return o * lax.rsqrt(jnp.mean(o * o, axis=-1, keepdims=True) + EPS) * g * post_scale


def _attn_prompt_kernel(lam_ref, q_ref, kb_ref, v_ref, g_ref, o_ref, vb_ref, *, tq, n_q, post_scale):
    qi = pl.program_id(2)

    @pl.when(qi == 0)
    def _():
        vb_ref[...] = v_ref[...].astype(BF16)

    lam = lam_ref[0]
    q = q_ref[...]
    for vi in range(n_q):
        @pl.when(qi == vi)
        def _(head=vi * tq, ext=(vi + 1) * tq):
            pos = lax.broadcasted_iota(jnp.int32, (tq, tq), 0)
            allowed = lax.broadcasted_iota(jnp.int32, (tq, tq), 1) <= (pos | (STREAM_CHUNK - 1))
            v_tail = vb_ref[head:ext, :]

            def one_map(c0):
                qc = q[:, c0:c0 + HEAD]
                st = jnp.where(allowed, _dot_nt(qc, kb_ref[head:ext, c0:c0 + HEAD]), -jnp.inf)
                m = jnp.max(st, axis=-1, keepdims=True)
                if head:
                    sh = _dot_nt(qc, kb_ref[0:head, c0:c0 + HEAD])
                    m = jnp.maximum(m, jnp.max(sh, axis=-1, keepdims=True))
                et = jnp.exp2(st - m)
                tot = jnp.sum(et, axis=-1, keepdims=True)
                acc = _dot(et.astype(BF16), v_tail)
                if head:
                    eh = jnp.exp2(sh - m)
                    tot = tot + jnp.sum(eh, axis=-1, keepdims=True)
                    acc = acc + _dot(eh.astype(BF16), vb_ref[0:head, :])
                return acc / tot

            o = one_map(0) - lam * one_map(HEAD)
            o_ref[...] = _subln(o, g_ref[...], post_scale).astype(o_ref.dtype)


def _attn_prompt(q, k, v, lam, subln, post_scale, bsz, t):
    d = q.shape[1]
    hw = 2 * HEAD
    nh = d // hw
    tq = _tile(t, ATTN_Q_ROWS)
    n_q = t // tq
    return pl.pallas_call(
        functools.partial(_attn_prompt_kernel, tq=tq, n_q=n_q, post_scale=post_scale),
        out_shape=jax.ShapeDtypeStruct((bsz * t, d), BF16),
        grid=(bsz, nh, n_q),
        in_specs=[pl.BlockSpec(memory_space=pltpu.SMEM),
                  pl.BlockSpec((tq, hw), lambda b, h, i: (b * n_q + i, h)),
                  pl.BlockSpec((t, hw), lambda b, h, i: (b, h)),
                  pl.BlockSpec((t, hw), lambda b, h, i: (b, h)),
                  pl.BlockSpec((1, hw), lambda b, h, i: (0, 0))],
        out_specs=pl.BlockSpec((tq, hw), lambda b, h, i: (b * n_q + i, h)),
        scratch_shapes=[pltpu.VMEM((t, hw), BF16)],
        compiler_params=_params(("parallel", "parallel", "arbitrary")),
        name="diff_attn_prompt",
    )(lam.reshape(1), q, k, v, subln.reshape(1, hw))


def _attn_sample_kernel(lam_ref, q_ref, kn_ref, vn_ref, ck_ref, cv_ref, g_ref, o_ref, *, hg, post_scale):
    hw = 2 * HEAD
    lam = lam_ref[0]
    past = ck_ref.shape[0]
    ck_rows = ck_ref.reshape(past * 2 * hg, HEAD)
    for h in range(hg):
        lanes = slice(h * hw, (h + 1) * hw)
        v = cv_ref[:, lanes].astype(BF16)
        vn = vn_ref[:, lanes].astype(BF16)
        outs = []
        for c in range(2):
            col = slice((2 * h + c) * HEAD, (2 * h + c + 1) * HEAD)
            qc = q_ref[:, col]
            s = _dot_nt(qc, ck_rows[pl.ds(2 * h + c, past, stride=2 * hg), :].astype(BF16))
            sn = _dot_nt(qc, kn_ref[:, col])
            m = jnp.maximum(jnp.max(s, axis=-1, keepdims=True), jnp.max(sn, axis=-1, keepdims=True))
            e, en = jnp.exp2(s - m), jnp.exp2(sn - m)
            tot = jnp.sum(e, axis=-1, keepdims=True) + jnp.sum(en, axis=-1, keepdims=True)
            outs.append((_dot(e.astype(BF16), v) + _dot(en.astype(BF16), vn)) / tot)
        o = outs[0] - lam * outs[1]
        o_ref[:, lanes] = _subln(o, g_ref[...], post_scale).astype(o_ref.dtype)


def _attn_sample(q, q_row0, k, v, cache_k, cache_v, lam, subln, post_scale, bsz, t):
    d = q.shape[1]
    hw = 2 * HEAD
    nh = d // hw
    past = cache_k.shape[1]
    hg = _tile(nh, 4)
    assert (2 * hg) % 8 == 0 or hg == nh
    assert q_row0 % t == 0
    blk0 = q_row0 // t
    new = pl.BlockSpec((t, hg * hw), lambda b, g: (b, g))
    return pl.pallas_call(
        functools.partial(_attn_sample_kernel, hg=hg, post_scale=post_scale),
        out_shape=jax.ShapeDtypeStruct((bsz * t, d), BF16),
        grid=(bsz, nh // hg),
        in_specs=[pl.BlockSpec(memory_space=pltpu.SMEM),
                  pl.BlockSpec((t, hg * hw), lambda b, g: (blk0 + b, g)), new, new,
                  pl.BlockSpec((past, 2 * hg, HEAD), lambda b, g: (b, g, 0)),
                  pl.BlockSpec((past, hg * hw), lambda b, g: (b, g)),
                  pl.BlockSpec((1, hw), lambda b, g: (0, 0))],
        out_specs=pl.BlockSpec((t, hg * hw), lambda b, g: (b, g)),
        compiler_params=_params(("parallel", "parallel")),
        name="diff_attn_sample",
    )(lam.reshape(1), q, k, v, cache_k.reshape(bsz * past, 2 * nh, HEAD), cache_v.reshape(bsz * past, d),
      subln.reshape(1, hw))


def _expert_changed(be_ref):
    b = pl.program_id(1)
    return (b == 0) | (be_ref[b] != be_ref[jnp.maximum(b - 1, 0)])


def _pack_halves(x):
    half = x.shape[1] // 2
    as_bits = lambda v: lax.bitcast_convert_type(v.astype(BF16).astype(F32), jnp.uint32)
    return (as_bits(x[:, :half]) >> 16) | (as_bits(x[:, half:]) & jnp.uint32(0xFFFF0000))


def _unpack_halves(u):
    lo = lax.bitcast_convert_type(u << 16, F32).astype(BF16)
    hi = lax.bitcast_convert_type(u & jnp.uint32(0xFFFF0000), F32).astype(BF16)
    return lo, hi


def _moe_dispatch_kernel(pos_ref, x_hbm, zero_hbm, xs_hbm, sem, *, rows):
    del zero_hbm
    t0 = pl.program_id(0) * rows
    copy = lambda t, k: pltpu.make_async_copy(x_hbm.at[t], xs_hbm.at[pos_ref[t * TOP_K + k]], sem)

    def issue(r, carry):
        for k in range(TOP_K):
            copy(t0 + r, k).start()
        return carry

    def drain(r, carry):
        for k in range(TOP_K):
            copy(t0 + r, k).wait()
        return carry

    lax.fori_loop(0, rows, issue, 0)
    lax.fori_loop(0, rows, drain, 0)


def _moe_dispatch(xq, pos, p):
    n, w = xq.shape
    rows = _tile(n, 512)
    return pl.pallas_call(
        functools.partial(_moe_dispatch_kernel, rows=rows),
        out_shape=jax.ShapeDtypeStruct((p, w), xq.dtype),
        grid_spec=pltpu.PrefetchScalarGridSpec(
            num_scalar_prefetch=1, grid=(n // rows,),
            in_specs=[pl.BlockSpec(memory_space=pl.ANY), pl.BlockSpec(memory_space=pl.ANY)],
            out_specs=pl.BlockSpec(memory_space=pl.ANY),
            scratch_shapes=[pltpu.SemaphoreType.DMA(())]),
        input_output_aliases={2: 0},
        compiler_params=_params(("arbitrary",)),
        name="moe_dispatch",
    )(pos.reshape(-1), xq, jnp.zeros((p, w), xq.dtype))


def _moe_combine_kernel(pos_ref, h_ref, g_ref, yb_hbm, o_ref, buf_ref, sem_ref, *, rows):
    i = pl.program_id(0)
    n_tiles = pl.num_programs(0)

    def copy(tile, r, k, slot):
        return pltpu.make_async_copy(yb_hbm.at[pos_ref[(tile * rows + r) * TOP_K + k]], buf_ref.at[slot, k, r],
                                     sem_ref.at[slot])

    def start_tile(tile, slot):
        def body(r, carry):
            for k in range(TOP_K):
                copy(tile, r, k, slot).start()
            return carry
        lax.fori_loop(0, rows, body, 0)

    @pl.when(i == 0)
    def _():
        start_tile(0, 0)

    @pl.when(i + 1 < n_tiles)
    def _():
        start_tile(i + 1, (i + 1) % 2)

    slot = i % 2

    def drain(r, carry):
        for k in range(TOP_K):
            copy(i, r, k, slot).wait()
        return carry

    lax.fori_loop(0, rows, drain, 0)
    g = g_ref[...]
    y = buf_ref[slot, 0] * g[:, 0:1]
    for k in range(1, TOP_K):
        y = y + buf_ref[slot, k] * g[:, k:k + 1]
    o_ref[...] = h_ref[...] + y


def _moe_combine(h, gates, yb, pos):
    n, d = h.shape
    rows = _tile(n, 128)
    return pl.pallas_call(
        functools.partial(_moe_combine_kernel, rows=rows),
        out_shape=jax.ShapeDtypeStruct((n, d), F32),
        grid_spec=pltpu.PrefetchScalarGridSpec(
            num_scalar_prefetch=1, grid=(n // rows,),
            in_specs=[pl.BlockSpec((rows, d), lambda i, pos: (i, 0)),
                      pl.BlockSpec((rows, TOP_K), lambda i, pos: (i, 0)),
                      pl.BlockSpec(memory_space=pl.ANY)],
            out_specs=pl.BlockSpec((rows, d), lambda i, pos: (i, 0)),
            scratch_shapes=[pltpu.VMEM((2, TOP_K, rows, d), F32), pltpu.SemaphoreType.DMA((2,))]),
        compiler_params=_params(("arbitrary",)),
        name="moe_combine",
    )(pos.reshape(-1), h, gates, yb)


def _moe_up_kernel(be_ref, x_ref, wg_ref, wu_ref, o_ref, wgb_ref, wub_ref):
    @pl.when(_expert_changed(be_ref))
    def _():
        wgb_ref[...] = wg_ref[...].astype(BF16)
        wub_ref[...] = wu_ref[...].astype(BF16)

    xl, xh = _unpack_halves(x_ref[...])
    half = xl.shape[1]
    g = _dot(xl, wgb_ref[0:half, :]) + _dot(xh, wgb_ref[half:, :])
    u = _dot(xl, wub_ref[0:half, :]) + _dot(xh, wub_ref[half:, :])
    o_ref[...] = (jax.nn.silu(g) * u).astype(o_ref.dtype)


def _moe_down_kernel(be_ref, h_ref, wd_ref, o_ref, wdb_ref):
    @pl.when(_expert_changed(be_ref))
    def _():
        wdb_ref[...] = wd_ref[...].astype(BF16)

    o_ref[...] = _dot(h_ref[...], wdb_ref[...])


def _moe_experts(xs, blk_exp, layer, w_gate, w_up, w_down):
    p, w = xs.shape
    d = 2 * w
    de = w_gate.shape[3]
    nb = p // MOE_ROWS
    te = _tile(de, 512)
    hidden = pl.pallas_call(
        _moe_up_kernel,
        out_shape=jax.ShapeDtypeStruct((p, de), BF16),
        grid_spec=pltpu.PrefetchScalarGridSpec(
            num_scalar_prefetch=1, grid=(de // te, nb),
            in_specs=[pl.BlockSpec((MOE_ROWS, w), lambda j, b, be: (b, 0)),
                      pl.BlockSpec((None, None, d, te), lambda j, b, be: (layer, be[b], 0, j)),
                      pl.BlockSpec((None, None, d, te), lambda j, b, be: (layer, be[b], 0, j))],
            out_specs=pl.BlockSpec((MOE_ROWS, te), lambda j, b, be: (b, j)),
            scratch_shapes=[pltpu.VMEM((d, te), BF16), pltpu.VMEM((d, te), BF16)]),
        compiler_params=_params(("arbitrary", "arbitrary")),
        name="moe_gate_up",
    )(blk_exp, xs, w_gate, w_up)
    tn = _tile(d, 2048)
    return pl.pallas_call(
        _moe_down_kernel,
        out_shape=jax.ShapeDtypeStruct((p, d), F32),
        grid_spec=pltpu.PrefetchScalarGridSpec(
            num_scalar_prefetch=1, grid=(d // tn, nb),
            in_specs=[pl.BlockSpec((MOE_ROWS, de), lambda j, b, be: (b, 0)),
                      pl.BlockSpec((None, None, de, tn), lambda j, b, be: (layer, be[b], 0, j))],
            out_specs=pl.BlockSpec((MOE_ROWS, tn), lambda j, b, be: (b, j)),
            scratch_shapes=[pltpu.VMEM((de, tn), BF16)]),
        compiler_params=_params(("arbitrary", "arbitrary")),
        name="moe_down",
    )(blk_exp, hidden, w_down)


def _top1(x):
    n = x.shape[-1]
    m = jnp.max(x, axis=-1, keepdims=True)
    idx = jnp.min(jnp.where(x == m, jnp.arange(n, dtype=jnp.int32), n), axis=-1, keepdims=True)
    return m, idx


def _route(logits, bg, be):
    n = logits.shape[0]
    pg = jax.nn.softmax(logits[:, :N_GROUPS] + bg, axis=-1)
    pg_top, g_top = _top1(pg)
    el = (logits[:, N_GROUPS:N_GROUPS + N_EXPERTS] + be).reshape(n, N_GROUPS, EXPERTS_PER_GROUP)
    sel = g_top[:, :, None] == jnp.arange(N_GROUPS, dtype=jnp.int32)[None, :, None]
    el_sel = jnp.sum(jnp.where(sel, el, 0.0), axis=1)
    pe = jax.nn.softmax(el_sel, axis=-1)
    p1, e1 = _top1(pe)
    lane = jnp.arange(EXPERTS_PER_GROUP, dtype=jnp.int32)[None, :]
    p2, e2 = _top1(jnp.where(lane == e1, -jnp.inf, pe))
    pe_top = jnp.concatenate([p1, p2], axis=-1)
    e_top = jnp.concatenate([e1, e2], axis=-1)
    gates = pg_top * pe_top / jnp.sum(pe_top, axis=-1, keepdims=True)
    return g_top * EXPERTS_PER_GROUP + e_top, gates


def _dispatch(expert_idx):
    n = expert_idx.shape[0]
    a = n * TOP_K
    flat_e = expert_idx.reshape(-1).astype(jnp.int32)
    onehot = (flat_e[:, None] == jnp.arange(N_EXPERTS, dtype=jnp.int32)[None, :]).astype(jnp.int32)
    seen = jnp.cumsum(onehot, axis=0)
    counts = seen[-1]
    padded = (counts + MOE_ROWS - 1) // MOE_ROWS * MOE_ROWS
    pend = jnp.cumsum(padded)
    pstart = pend - padded
    n_blocks = -(-(a + N_EXPERTS * (MOE_ROWS - 1)) // MOE_ROWS)
    blk_start = jnp.arange(n_blocks, dtype=jnp.int32) * MOE_ROWS
    blk_exp = jnp.minimum(jnp.searchsorted(pend, blk_start, side='right'), N_EXPERTS - 1).astype(jnp.int32)
    pos = jnp.sum(onehot * (seen - 1 + pstart[None, :]), axis=1)
    return n_blocks * MOE_ROWS, blk_exp, pos.reshape(n, TOP_K)


def kernel(x_prompt, x_sample, state_hgrn, cache_k, cache_v, p_prompt, p_sample, norm_mix, norm_ffn, norm_ple, norm_final, a_w_in, a_w_out, a_lb_logits, a_gnorm, b_w_in, b_w_out, b_lambda_q1, b_lambda_k1, b_lambda_q2, b_lambda_k2, b_subln, router_group_w, router_group_b, router_expert_w, router_expert_b, expert_w_gate, expert_w_up, expert_w_down, ple_w_up, ple_w_gate):
    bp, tp, d = x_prompt.shape
    bs, ts, _ = x_sample.shape
    n_p, n_s = bp * tp, bs * ts
    depth = norm_mix.shape[0]
    nh_b = d // (2 * HEAD)

    h = jnp.concatenate([x_prompt.reshape(n_p, d), x_sample.reshape(n_s, d)], axis=0)
    lb_all = jnp.cumsum(jax.nn.softmax(a_lb_logits.astype(F32), axis=0), axis=0)
    states_p, states_s, k_p, v_p, k_s, v_s = [], [], [], [], [], []

    for i in range(depth):
        j = i // 2
        xn = _rmsnorm(h, norm_mix[i], BF16)
        if i % 2 == 0:
            proj = _matmul(xn, a_w_in[j].astype(BF16))
            o_p, s_p = _hgrn2(proj, 0, bp, tp, lb_all[j], a_gnorm[j], None)
            o_s, s_s = _hgrn2(proj, n_p, bs, ts, lb_all[j], a_gnorm[j], state_hgrn[j])
            states_p.append(s_p)
            states_s.append(s_s)
            w_out = a_w_out[j]
        else:
            w_in = b_w_in[j]
            wk, wv = w_in[:, d:2 * d].astype(BF16), w_in[:, 2 * d:].astype(BF16)
            q = _matmul(xn, w_in[:, :d].astype(BF16), out_dtype=BF16, scale=HEAD ** -0.5 * LOG2E)
            (kp, kp16), vp = _matmul_rowsplit(xn, wk, rows=n_p), _matmul(xn, wv, rows=n_p)
            (ks, ks16), vs = _matmul_rowsplit(xn, wk, row0=n_p, rows=n_s), _matmul(xn, wv, row0=n_p, rows=n_s)
            lam0 = 0.8 - 0.6 * math.exp(-0.3 * i)
            lam = (jnp.exp(jnp.sum(b_lambda_q1[j] * b_lambda_k1[j]))
                   - jnp.exp(jnp.sum(b_lambda_q2[j] * b_lambda_k2[j])) + lam0).astype(F32)
            o_p = _attn_prompt(q, kp16, vp, lam, b_subln[j], 1.0 - lam0, bp, tp)
            o_s = _attn_sample(q, n_p, ks16, vs, cache_k[j], cache_v[j], lam, b_subln[j], 1.0 - lam0, bs, ts)
            k_p.append(kp.reshape(bp, tp, nh_b, 2, HEAD))
            v_p.append(vp.reshape(bp, tp, nh_b, 2 * HEAD))
            k_s.append(ks.reshape(bs, ts, nh_b, 2, HEAD))
            v_s.append(vs.reshape(bs, ts, nh_b, 2 * HEAD))
            w_out = b_w_out[j]
        h = _matmul_residual2(o_p, o_s, w_out.astype(BF16), h)

        w_router = jnp.zeros((d, ROUTER_LANES), F32)
        w_router = w_router.at[:, :N_GROUPS].set(router_group_w[i])
        w_router = w_router.at[:, N_GROUPS:N_GROUPS + N_EXPERTS].set(router_expert_w[i])
        xq, logits = _rmsnorm_router(h, norm_ffn[i], w_router)
        expert_idx, gates = _route(logits, router_group_b[i], router_expert_b[i])
        n_slots, blk_exp, pos = _dispatch(expert_idx)
        yb = _moe_experts(_moe_dispatch(xq, pos, n_slots), blk_exp, i, expert_w_gate, expert_w_up, expert_w_down)
        h = _moe_combine(h, gates, yb, pos)

        p_i = jnp.concatenate([p_prompt[i].reshape(n_p, -1), p_sample[i].reshape(n_s, -1)], axis=0)
        xn = _rmsnorm(h, norm_ple[i], BF16)
        h = _matmul_ple(xn, ple_w_gate[i].astype(BF16), h, p_i.astype(BF16), ple_w_up[i].astype(BF16))

    y_p = _rmsnorm(h, norm_final, F32, 0, n_p)
    y_s = _rmsnorm(h, norm_final, F32, n_p, n_s)
    return (y_p.reshape(bp, tp, d), y_s.reshape(bs, ts, d), jnp.stack(states_p), jnp.stack(states_s),
            jnp.stack(k_p), jnp.stack(v_p), jnp.stack(k_s), jnp.stack(v_s))
```

```python
import functools
import math

import jax
import jax.numpy as jnp
import numpy as np
from jax import lax
from jax.experimental import pallas as pl
from jax.experimental.pallas import tpu as pltpu

F32 = jnp.float32
BF16 = jnp.bfloat16

EPS = 1e-6
HEAD = 128
STREAM_CHUNK = 64
N_GROUPS = 4
EXPERTS_PER_GROUP = 8
N_EXPERTS = N_GROUPS * EXPERTS_PER_GROUP
TOP_K = 2
ROUTER_LANES = 128
MOE_ROWS = 256
HGRN2_HEADS_PER_STEP = 8
DMA_ISSUE_UNROLL = 8
ATTN_Q_ROWS = 512
V7X_VMEM_LIMIT = 56 * 1024 * 1024
LOG2E = math.log2(math.e)


def _tile(dim, want):
    t = min(dim, want)
    while dim % t:
        t //= 2
    return t


def _params(sem):
    return pltpu.CompilerParams(dimension_semantics=sem, vmem_limit_bytes=V7X_VMEM_LIMIT)


def _split3(x):
    hi = x.astype(BF16)
    r1 = x - hi.astype(F32)
    mid = r1.astype(BF16)
    lo = (r1 - mid.astype(F32)).astype(BF16)
    return hi, mid, lo


def _dot(a, b):
    return jnp.dot(a, b, preferred_element_type=F32)


def _dot_nt(a, b):
    return lax.dot_general(a, b, (((1,), (1,)), ((), ())), preferred_element_type=F32)


def _rmsnorm_kernel(x_ref, g_ref, o_ref):
    x = x_ref[...]
    y = x * lax.rsqrt(jnp.mean(x * x, axis=-1, keepdims=True) + EPS)
    o_ref[...] = (y * g_ref[...]).astype(o_ref.dtype)


def _rmsnorm(x, g, out_dtype, row0=0, rows=None):
    n, d = x.shape
    rows = n if rows is None else rows
    tm = _tile(math.gcd(rows, row0) if row0 else rows, 512)
    blk0 = row0 // tm
    return pl.pallas_call(
        _rmsnorm_kernel,
        out_shape=jax.ShapeDtypeStruct((rows, d), out_dtype),
        grid=(rows // tm,),
        in_specs=[pl.BlockSpec((tm, d), lambda i: (blk0 + i, 0)), pl.BlockSpec((1, d), lambda i: (0, 0))],
        out_specs=pl.BlockSpec((tm, d), lambda i: (i, 0)),
        compiler_params=_params(("parallel",)),
        name="rmsnorm",
    )(x, g.reshape(1, d))


def _rmsnorm_router_kernel(x_ref, g_ref, w0_ref, w1_ref, w2_ref, o_ref, l_ref):
    x = x_ref[...]
    y = x * lax.rsqrt(jnp.mean(x * x, axis=-1, keepdims=True) + EPS)
    xn = y * g_ref[...]
    o_ref[...] = _pack_halves(xn)
    x0, x1, x2 = _split3(xn)
    w0, w1, w2 = w0_ref[...], w1_ref[...], w2_ref[...]
    small = _dot(x0, w2) + _dot(x1, w1) + _dot(x2, w0)
    mid = _dot(x0, w1) + _dot(x1, w0)
    l_ref[...] = _dot(x0, w0) + (mid + small)


def _rmsnorm_router(x, g, w_router):
    n, d = x.shape
    tm = _tile(n, 512)
    w0, w1, w2 = _split3(w_router)
    wspec = pl.BlockSpec((d, ROUTER_LANES), lambda i: (0, 0))
    return pl.pallas_call(
        _rmsnorm_router_kernel,
        out_shape=(jax.ShapeDtypeStruct((n, d // 2), jnp.uint32), jax.ShapeDtypeStruct((n, ROUTER_LANES), F32)),
        grid=(n // tm,),
        in_specs=[pl.BlockSpec((tm, d), lambda i: (i, 0)), pl.BlockSpec((1, d), lambda i: (0, 0)),
                  wspec, wspec, wspec],
        out_specs=(pl.BlockSpec((tm, d // 2), lambda i: (i, 0)), pl.BlockSpec((tm, ROUTER_LANES), lambda i: (i, 0))),
        compiler_params=_params(("parallel",)),
        name="rmsnorm_router",
    )(x, g.reshape(1, d), w0, w1, w2)


def _mm_kernel(x_ref, w_ref, o_ref, *, scale):
    acc = _dot(x_ref[...], w_ref[...])
    if scale is not None:
        acc = acc * scale
    o_ref[...] = acc.astype(o_ref.dtype)


def _mm_rowsplit_kernel(x_ref, w_ref, o_ref, o16_ref):
    acc = _dot(x_ref[...], w_ref[...])
    o16_ref[...] = acc.astype(o16_ref.dtype)
    for c in range(o_ref.shape[1]):
        o_ref[:, c, :] = acc[:, c * HEAD:(c + 1) * HEAD]


def _matmul_rowsplit(x, w, *, row0=0, rows=None, tm=1024, tn=1024):
    m, k = x.shape
    rows = m if rows is None else rows
    n = w.shape[1]
    tm = _tile(math.gcd(rows, row0) if row0 else rows, tm)
    tn = _tile(n, tn)
    blk0 = row0 // tm
    return pl.pallas_call(
        _mm_rowsplit_kernel,
        out_shape=(jax.ShapeDtypeStruct((rows, n // HEAD, HEAD), F32), jax.ShapeDtypeStruct((rows, n), BF16)),
        grid=(rows // tm, n // tn),
        in_specs=[pl.BlockSpec((tm, k), lambda i, j: (blk0 + i, 0)), pl.BlockSpec((k, tn), lambda i, j: (0, j))],
        out_specs=(pl.BlockSpec((tm, tn // HEAD, HEAD), lambda i, j: (i, j, 0)),
                   pl.BlockSpec((tm, tn), lambda i, j: (i, j))),
        compiler_params=_params(("parallel", "arbitrary")),
        name="matmul_rowsplit",
    )(x, w)


def _mm_res2_kernel(xa_ref, xb_ref, w_ref, r_ref, o_ref, *, a_tiles):
    i = pl.program_id(0)

    @pl.when(i < a_tiles)
    def _():
        o_ref[...] = r_ref[...] + _dot(xa_ref[...], w_ref[...])

    @pl.when(i >= a_tiles)
    def _():
        o_ref[...] = r_ref[...] + _dot(xb_ref[...], w_ref[...])


def _mm_ple_kernel(x_ref, w_ref, r_ref, p_ref, wup_ref, o_ref):
    gate = jax.nn.sigmoid(_dot(x_ref[...], w_ref[...]))
    o_ref[...] = r_ref[...] + _dot(p_ref[...], wup_ref[...]) * gate


def _matmul(x, w, *, out_dtype=F32, scale=None, row0=0, rows=None, tm=1024, tn=1024):
    m, k = x.shape
    rows = m if rows is None else rows
    n = w.shape[1]
    tm = _tile(math.gcd(rows, row0) if row0 else rows, tm)
    tn = _tile(n, tn)
    blk0 = row0 // tm
    return pl.pallas_call(
        functools.partial(_mm_kernel, scale=scale),
        out_shape=jax.ShapeDtypeStruct((rows, n), out_dtype),
        grid=(rows // tm, n // tn),
        in_specs=[pl.BlockSpec((tm, k), lambda i, j: (blk0 + i, 0)), pl.BlockSpec((k, tn), lambda i, j: (0, j))],
        out_specs=pl.BlockSpec((tm, tn), lambda i, j: (i, j)),
        compiler_params=_params(("parallel", "arbitrary")),
        name="matmul",
    )(x, w)


def _matmul_residual2(xa, xb, w, res, *, tm=1024, tn=512):
    ma, k = xa.shape
    mb = xb.shape[0]
    n = w.shape[1]
    tm, tn = _tile(math.gcd(ma, mb), tm), _tile(n, tn)
    a_tiles, b_tiles = ma // tm, mb // tm
    return pl.pallas_call(
        functools.partial(_mm_res2_kernel, a_tiles=a_tiles),
        out_shape=jax.ShapeDtypeStruct((ma + mb, n), F32),
        grid=(a_tiles + b_tiles, n // tn),
        in_specs=[pl.BlockSpec((tm, k), lambda i, j: (jnp.minimum(i, a_tiles - 1), 0)),
                  pl.BlockSpec((tm, k), lambda i, j: (jnp.maximum(i - a_tiles, 0), 0)),
                  pl.BlockSpec((k, tn), lambda i, j: (0, j)),
                  pl.BlockSpec((tm, tn), lambda i, j: (i, j))],
        out_specs=pl.BlockSpec((tm, tn), lambda i, j: (i, j)),
        compiler_params=_params(("parallel", "arbitrary")),
        name="matmul_residual",
    )(xa, xb, w, res)


def _matmul_ple(xn, w_gate, res, p, w_up, *, tm=1024, tn=512):
    m, k = xn.shape
    n = w_gate.shape[1]
    kp = p.shape[1]
    tm, tn = _tile(m, tm), _tile(n, tn)
    return pl.pallas_call(
        _mm_ple_kernel,
        out_shape=jax.ShapeDtypeStruct((m, n), F32),
        grid=(m // tm, n // tn),
        in_specs=[pl.BlockSpec((tm, k), lambda i, j: (i, 0)), pl.BlockSpec((k, tn), lambda i, j: (0, j)),
                  pl.BlockSpec((tm, tn), lambda i, j: (i, j)),
                  pl.BlockSpec((tm, kp), lambda i, j: (i, 0)), pl.BlockSpec((kp, tn), lambda i, j: (0, j))],
        out_specs=pl.BlockSpec((tm, tn), lambda i, j: (i, j)),
        compiler_params=_params(("parallel", "arbitrary")),
        name="matmul_ple",
    )(xn, w_gate, res, p, w_up)


def _hgrn2_level_table(c):
    t = np.arange(c)[:, None]
    s = np.arange(c)[None, :]
    x = np.bitwise_xor(t, s)
    lvl = np.where(x > 0, np.floor(np.log2(np.maximum(x, 1))).astype(np.int32), -1)
    return np.where(s < t, lvl, -1).astype(np.int32)


def _hgrn2_sign_table(c):
    t = np.arange(c)[None, :, None]
    j = np.arange(int(math.log2(c)))[:, None, None]
    return np.broadcast_to(np.where((t >> j) & 1, 1.0, -1.0), (j.shape[0], c, HEAD)).astype(np.float32)


def _hgrn2_kernel(q_ref, f_ref, v_ref, g_ref, lb_ref, gn_ref, lvl_ref, tri_ref, sgn_ref, *rest, chunk, n_chunks, hp,
                  has_s0):
    if has_s0:
        s0_ref, o_ref, s_ref, st_ref, b_ref = rest
    else:
        o_ref, s_ref, st_ref, b_ref = rest
    tb = pl.program_id(2)
    c = chunk

    @pl.when(tb == 0)
    def _():
        for hh in range(hp):
            st_ref[hh] = s0_ref[0, hh].T if has_s0 else jnp.zeros((HEAD, HEAD), F32)

    gn = gn_ref[...]
    lvl = lvl_ref[...]
    tri = tri_ref[...]
    sub = lax.broadcasted_iota(jnp.int32, (8, HEAD), 0)
    n_levels = int(math.log2(c))

    def one_head(rows, hh):
        lanes = slice(hh * HEAD, (hh + 1) * HEAD)
        lb = lb_ref[:, lanes]
        q = jax.nn.silu(q_ref[rows, lanes])
        f = lb + (1.0 - lb) * jax.nn.sigmoid(f_ref[rows, lanes])
        k = 1.0 - f
        v = v_ref[rows, lanes]
        lg = jnp.log(f) * LOG2E
        l0, l1, l2 = _split3(lg)
        cs = _dot(tri, jnp.concatenate([l0, l1, l2], axis=1))
        b = cs[:, :HEAD] + (cs[:, HEAD:2 * HEAD] + cs[:, 2 * HEAD:])
        b_ref[hh] = b
        k16 = k.astype(BF16)

        att = jnp.where(lvl == 0, _dot_nt((q * f).astype(BF16), k16), 0.0)
        for j in range(1, n_levels):
            m = 1 << j
            pieces = []
            for g8 in range(c // 8):
                if m >= 4:
                    r = (g8 * 8 // (2 * m)) * 2 * m + m - 1
                    pieces.append(jnp.broadcast_to(b_ref[hh, r:r + 1, :], (8, HEAD)))
                else:
                    top = jnp.broadcast_to(b_ref[hh, g8 * 8 + 1:g8 * 8 + 2, :], (8, HEAD))
                    bot = jnp.broadcast_to(b_ref[hh, g8 * 8 + 5:g8 * 8 + 6, :], (8, HEAD))
                    pieces.append(jnp.where(sub < 4, top, bot))
            e = jnp.exp2((b - jnp.concatenate(pieces, axis=0)) * sgn_ref[j])
            sj = _dot_nt((q * e).astype(BF16), (k * e).astype(BF16))
            att = jnp.where(lvl == j, sj, att)

        st = st_ref[hh]
        b_end = b_ref[hh, c - 1:c, :]
        o = _dot(att.astype(BF16), v.astype(BF16))
        o = o + _dot_nt((q * jnp.exp2(b)).astype(BF16), st.astype(BF16))
        o = o + jnp.sum(q * k, axis=-1, keepdims=True) * v
        kd = (k * jnp.exp2(b_end - b)).astype(BF16)
        st_ref[hh] = jnp.exp2(b_end) * st + _dot(v.T.astype(BF16), kd)

        o = o * lax.rsqrt(jnp.mean(o * o, axis=-1, keepdims=True) + EPS) * gn
        o_ref[rows, lanes] = (o * jax.nn.silu(g_ref[rows, lanes])).astype(o_ref.dtype)

    def one_chunk(ci, carry):
        rows = pl.ds(pl.multiple_of(ci * c, c), c)
        for hh in range(hp):
            one_head(rows, hh)
        return carry

    lax.fori_loop(0, n_chunks, one_chunk, 0)

    @pl.when(tb == pl.num_programs(2) - 1)
    def _():
        for hh in range(hp):
            s_ref[0, hh] = st_ref[hh].T


def _hgrn2(proj, row_off, bsz, t, lb, gnorm, s0):
    d = proj.shape[1] // 4
    nh = d // HEAD
    hp = _tile(nh, HGRN2_HEADS_PER_STEP)
    c = min(t, 128)
    n_levels = int(math.log2(c))
    tb = _tile(t, 512)
    n_tb = t // tb
    assert row_off % tb == 0 and tb % c == 0
    blk0 = row_off // tb
    ng = nh // hp
    col = lambda kind: pl.BlockSpec((tb, hp * HEAD), lambda b, h, i: (blk0 + b * n_tb + i, kind * ng + h))
    const = lambda shape: pl.BlockSpec(shape, lambda b, h, i: (0, 0))
    st_spec = pl.BlockSpec((1, hp, HEAD, HEAD), lambda b, h, i: (b, h, 0, 0))
    in_specs = [col(0), col(1), col(2), col(3), pl.BlockSpec((1, hp * HEAD), lambda b, h, i: (0, h)),
                const((1, HEAD)), const((c, c)), const((c, c)),
                pl.BlockSpec((n_levels, c, HEAD), lambda b, h, i: (0, 0, 0))]
    args = [proj, proj, proj, proj, lb.reshape(1, d), gnorm.reshape(1, HEAD),
            jnp.asarray(_hgrn2_level_table(c)), jnp.asarray(np.tril(np.ones((c, c), np.float32)), BF16),
            jnp.asarray(_hgrn2_sign_table(c))]
    if s0 is not None:
        in_specs.append(st_spec)
        args.append(s0)
    return pl.pallas_call(
        functools.partial(_hgrn2_kernel, chunk=c, n_chunks=tb // c, hp=hp, has_s0=s0 is not None),
        out_shape=(jax.ShapeDtypeStruct((bsz * t, d), BF16), jax.ShapeDtypeStruct((bsz, nh, HEAD, HEAD), F32)),
        grid=(bsz, ng, n_tb),
        in_specs=in_specs,
        out_specs=(pl.BlockSpec((tb, hp * HEAD), lambda b, h, i: (b * n_tb + i, h)), st_spec),
        scratch_shapes=[pltpu.VMEM((hp, HEAD, HEAD), F32), pltpu.VMEM((hp, c, HEAD), F32)],
        compiler_params=_params(("parallel", "parallel", "arbitrary")),
        name="hgrn2_scan",
    )(*args)


def _subln(o, g, post_scale):
    return o * lax.rsqrt(jnp.mean(o * o, axis=-1, keepdims=True) + EPS) * g * post_scale


def _attn_prompt_kernel(lam_ref, q_ref, kb_ref, v_ref, g_ref, o_ref, vb_ref, *, tq, n_q, post_scale):
    qi = pl.program_id(2)

    @pl.when(qi == 0)
    def _():
        vb_ref[...] = v_ref[...].astype(BF16)

    lam = lam_ref[0]
    q = q_ref[...]
    for vi in range(n_q):
        @pl.when(qi == vi)
        def _(head=vi * tq, ext=(vi + 1) * tq):
            pos = lax.broadcasted_iota(jnp.int32, (tq, tq), 0)
            allowed = lax.broadcasted_iota(jnp.int32, (tq, tq), 1) <= (pos | (STREAM_CHUNK - 1))
            v_tail = vb_ref[head:ext, :]

            def one_map(c0):
                qc = q[:, c0:c0 + HEAD]
                st = jnp.where(allowed, _dot_nt(qc, kb_ref[head:ext, c0:c0 + HEAD]), -jnp.inf)
                m = jnp.max(st, axis=-1, keepdims=True)
                if head:
                    sh = _dot_nt(qc, kb_ref[0:head, c0:c0 + HEAD])
                    m = jnp.maximum(m, jnp.max(sh, axis=-1, keepdims=True))
                et = jnp.exp2(st - m)
                tot = jnp.sum(et, axis=-1, keepdims=True)
                acc = _dot(et.astype(BF16), v_tail)
                if head:
                    eh = jnp.exp2(sh - m)
                    tot = tot + jnp.sum(eh, axis=-1, keepdims=True)
                    acc = acc + _dot(eh.astype(BF16), vb_ref[0:head, :])
                return acc / tot

            o = one_map(0) - lam * one_map(HEAD)
            o_ref[...] = _subln(o, g_ref[...], post_scale).astype(o_ref.dtype)


def _attn_prompt(q, k, v, lam, subln, post_scale, bsz, t):
    d = q.shape[1]
    hw = 2 * HEAD
    nh = d // hw
    tq = _tile(t, ATTN_Q_ROWS)
    n_q = t // tq
    return pl.pallas_call(
        functools.partial(_attn_prompt_kernel, tq=tq, n_q=n_q, post_scale=post_scale),
        out_shape=jax.ShapeDtypeStruct((bsz * t, d), BF16),
        grid=(bsz, nh, n_q),
        in_specs=[pl.BlockSpec(memory_space=pltpu.SMEM),
                  pl.BlockSpec((tq, hw), lambda b, h, i: (b * n_q + i, h)),
                  pl.BlockSpec((t, hw), lambda b, h, i: (b, h)),
                  pl.BlockSpec((t, hw), lambda b, h, i: (b, h)),
                  pl.BlockSpec((1, hw), lambda b, h, i: (0, 0))],
        out_specs=pl.BlockSpec((tq, hw), lambda b, h, i: (b * n_q + i, h)),
        scratch_shapes=[pltpu.VMEM((t, hw), BF16)],
        compiler_params=_params(("parallel", "parallel", "arbitrary")),
        name="diff_attn_prompt",
    )(lam.reshape(1), q, k, v, subln.reshape(1, hw))


def _attn_sample_kernel(lam_ref, q_ref, kn_ref, vn_ref, ck_ref, cv_ref, g_ref, o_ref, *, hg, post_scale):
    hw = 2 * HEAD
    lam = lam_ref[0]
    past = ck_ref.shape[0]
    ck_rows = ck_ref.reshape(past * 2 * hg, HEAD)
    for h in range(hg):
        lanes = slice(h * hw, (h + 1) * hw)
        v = cv_ref[:, lanes].astype(BF16)
        vn = vn_ref[:, lanes].astype(BF16)
        outs = []
        for c in range(2):
            col = slice((2 * h + c) * HEAD, (2 * h + c + 1) * HEAD)
            qc = q_ref[:, col]
            s = _dot_nt(qc, ck_rows[pl.ds(2 * h + c, past, stride=2 * hg), :].astype(BF16))
            sn = _dot_nt(qc, kn_ref[:, col])
            m = jnp.maximum(jnp.max(s, axis=-1, keepdims=True), jnp.max(sn, axis=-1, keepdims=True))
            e, en = jnp.exp2(s - m), jnp.exp2(sn - m)
            tot = jnp.sum(e, axis=-1, keepdims=True) + jnp.sum(en, axis=-1, keepdims=True)
            outs.append((_dot(e.astype(BF16), v) + _dot(en.astype(BF16), vn)) / tot)
        o = outs[0] - lam * outs[1]
        o_ref[:, lanes] = _subln(o, g_ref[...], post_scale).astype(o_ref.dtype)


def _attn_sample(q, q_row0, k, v, cache_k, cache_v, lam, subln, post_scale, bsz, t):
    d = q.shape[1]
    hw = 2 * HEAD
    nh = d // hw
    past = cache_k.shape[1]
    hg = _tile(nh, 4)
    assert (2 * hg) % 8 == 0 or hg == nh
    assert q_row0 % t == 0
    blk0 = q_row0 // t
    new = pl.BlockSpec((t, hg * hw), lambda b, g: (b, g))
    return pl.pallas_call(
        functools.partial(_attn_sample_kernel, hg=hg, post_scale=post_scale),
        out_shape=jax.ShapeDtypeStruct((bsz * t, d), BF16),
        grid=(bsz, nh // hg),
        in_specs=[pl.BlockSpec(memory_space=pltpu.SMEM),
                  pl.BlockSpec((t, hg * hw), lambda b, g: (blk0 + b, g)), new, new,
                  pl.BlockSpec((past, 2 * hg, HEAD), lambda b, g: (b, g, 0)),
                  pl.BlockSpec((past, hg * hw), lambda b, g: (b, g)),
                  pl.BlockSpec((1, hw), lambda b, g: (0, 0))],
        out_specs=pl.BlockSpec((t, hg * hw), lambda b, g: (b, g)),
        compiler_params=_params(("parallel", "parallel")),
        name="diff_attn_sample",
    )(lam.reshape(1), q, k, v, cache_k.reshape(bsz * past, 2 * nh, HEAD), cache_v.reshape(bsz * past, d),
      subln.reshape(1, hw))


def _expert_changed(be_ref):
    b = pl.program_id(1)
    return (b == 0) | (be_ref[b] != be_ref[jnp.maximum(b - 1, 0)])


def _pack_halves(x):
    half = x.shape[1] // 2
    as_bits = lambda v: lax.bitcast_convert_type(v.astype(BF16).astype(F32), jnp.uint32)
    return (as_bits(x[:, :half]) >> 16) | (as_bits(x[:, half:]) & jnp.uint32(0xFFFF0000))


def _unpack_halves(u):
    lo = lax.bitcast_convert_type(u << 16, F32).astype(BF16)
    hi = lax.bitcast_convert_type(u & jnp.uint32(0xFFFF0000), F32).astype(BF16)
    return lo, hi


def _moe_dispatch_kernel(tok_ref, x_hbm, o_ref, buf_ref, sem_ref, *, rows):
    i = pl.program_id(0)

    def copy(tile, r, slot):
        return pltpu.make_async_copy(x_hbm.at[tok_ref[tile * rows + r]], buf_ref.at[slot, r], sem_ref.at[slot])

    def start_tile(tile, slot):
        lax.fori_loop(0, rows, lambda r, c: (copy(tile, r, slot).start(), c)[1], 0, unroll=DMA_ISSUE_UNROLL)

    @pl.when(i == 0)
    def _():
        start_tile(0, 0)

    @pl.when(i + 1 < pl.num_programs(0))
    def _():
        start_tile(i + 1, (i + 1) % 2)

    slot = i % 2
    lax.fori_loop(0, rows, lambda r, c: (copy(i, r, slot).wait(), c)[1], 0, unroll=DMA_ISSUE_UNROLL)
    o_ref[...] = buf_ref[slot]


def _moe_dispatch(xq, slot_tok):
    w = xq.shape[1]
    p = slot_tok.shape[0]
    rows = MOE_ROWS
    return pl.pallas_call(
        functools.partial(_moe_dispatch_kernel, rows=rows),
        out_shape=jax.ShapeDtypeStruct((p, w), xq.dtype),
        grid_spec=pltpu.PrefetchScalarGridSpec(
            num_scalar_prefetch=1, grid=(p // rows,),
            in_specs=[pl.BlockSpec(memory_space=pl.ANY)],
            out_specs=pl.BlockSpec((rows, w), lambda i, tok: (i, 0)),
            scratch_shapes=[pltpu.VMEM((2, rows, w), xq.dtype), pltpu.SemaphoreType.DMA((2,))]),
        compiler_params=_params(("arbitrary",)),
        name="moe_dispatch",
    )(slot_tok, xq)


def _moe_combine_kernel(pos_ref, h_ref, g_ref, yb_hbm, o_ref, buf_ref, sem_ref, *, rows):
    i = pl.program_id(0)
    n_tiles = pl.num_programs(0)

    def copy(tile, r, k, slot):
        return pltpu.make_async_copy(yb_hbm.at[pos_ref[(tile * rows + r) * TOP_K + k]], buf_ref.at[slot, k, r],
                                     sem_ref.at[slot])

    def start_tile(tile, slot):
        def body(r, carry):
            for k in range(TOP_K):
                copy(tile, r, k, slot).start()
            return carry
        lax.fori_loop(0, rows, body, 0, unroll=DMA_ISSUE_UNROLL)

    @pl.when(i == 0)
    def _():
        start_tile(0, 0)

    @pl.when(i + 1 < n_tiles)
    def _():
        start_tile(i + 1, (i + 1) % 2)

    slot = i % 2

    def drain(r, carry):
        for k in range(TOP_K):
            copy(i, r, k, slot).wait()
        return carry

    lax.fori_loop(0, rows, drain, 0, unroll=DMA_ISSUE_UNROLL)
    g = g_ref[...]
    y = buf_ref[slot, 0] * g[:, 0:1]
    for k in range(1, TOP_K):
        y = y + buf_ref[slot, k] * g[:, k:k + 1]
    o_ref[...] = h_ref[...] + y


def _moe_combine(h, gates, yb, pos):
    n, d = h.shape
    rows = _tile(n, 128)
    return pl.pallas_call(
        functools.partial(_moe_combine_kernel, rows=rows),
        out_shape=jax.ShapeDtypeStruct((n, d), F32),
        grid_spec=pltpu.PrefetchScalarGridSpec(
            num_scalar_prefetch=1, grid=(n // rows,),
            in_specs=[pl.BlockSpec((rows, d), lambda i, pos: (i, 0)),
                      pl.BlockSpec((rows, TOP_K), lambda i, pos: (i, 0)),
                      pl.BlockSpec(memory_space=pl.ANY)],
            out_specs=pl.BlockSpec((rows, d), lambda i, pos: (i, 0)),
            scratch_shapes=[pltpu.VMEM((2, TOP_K, rows, d), F32), pltpu.SemaphoreType.DMA((2,))]),
        compiler_params=_params(("arbitrary",)),
        name="moe_combine",
    )(pos.reshape(-1), h, gates, yb)


def _moe_up_kernel(be_ref, x_ref, wg_ref, wu_ref, o_ref, wgb_ref, wub_ref):
    @pl.when(_expert_changed(be_ref))
    def _():
        wgb_ref[...] = wg_ref[...].astype(BF16)
        wub_ref[...] = wu_ref[...].astype(BF16)

    xl, xh = _unpack_halves(x_ref[...])
    half = xl.shape[1]
    g = _dot(xl, wgb_ref[0:half, :]) + _dot(xh, wgb_ref[half:, :])
    u = _dot(xl, wub_ref[0:half, :]) + _dot(xh, wub_ref[half:, :])
    o_ref[...] = (jax.nn.silu(g) * u).astype(o_ref.dtype)


def _moe_down_kernel(be_ref, h_ref, wd_ref, o_ref, wdb_ref):
    @pl.when(_expert_changed(be_ref))
    def _():
        wdb_ref[...] = wd_ref[...].astype(BF16)

    o_ref[...] = _dot(h_ref[...], wdb_ref[...])


def _moe_experts(xs, blk_exp, layer, w_gate, w_up, w_down):
    p, w = xs.shape
    d = 2 * w
    de = w_gate.shape[3]
    nb = p // MOE_ROWS
    te = _tile(de, 512)
    hidden = pl.pallas_call(
        _moe_up_kernel,
        out_shape=jax.ShapeDtypeStruct((p, de), BF16),
        grid_spec=pltpu.PrefetchScalarGridSpec(
            num_scalar_prefetch=1, grid=(de // te, nb),
            in_specs=[pl.BlockSpec((MOE_ROWS, w), lambda j, b, be: (b, 0)),
                      pl.BlockSpec((None, None, d, te), lambda j, b, be: (layer, be[b], 0, j)),
                      pl.BlockSpec((None, None, d, te), lambda j, b, be: (layer, be[b], 0, j))],
            out_specs=pl.BlockSpec((MOE_ROWS, te), lambda j, b, be: (b, j)),
            scratch_shapes=[pltpu.VMEM((d, te), BF16), pltpu.VMEM((d, te), BF16)]),
        compiler_params=_params(("arbitrary", "arbitrary")),
        name="moe_gate_up",
    )(blk_exp, xs, w_gate, w_up)
    tn = _tile(d, 2048)
    return pl.pallas_call(
        _moe_down_kernel,
        out_shape=jax.ShapeDtypeStruct((p, d), F32),
        grid_spec=pltpu.PrefetchScalarGridSpec(
            num_scalar_prefetch=1, grid=(d // tn, nb),
            in_specs=[pl.BlockSpec((MOE_ROWS, de), lambda j, b, be: (b, 0)),
                      pl.BlockSpec((None, None, de, tn), lambda j, b, be: (layer, be[b], 0, j))],
            out_specs=pl.BlockSpec((MOE_ROWS, tn), lambda j, b, be: (b, j)),
            scratch_shapes=[pltpu.VMEM((de, tn), BF16)]),
        compiler_params=_params(("arbitrary", "arbitrary")),
        name="moe_down",
    )(blk_exp, hidden, w_down)


def _top1(x):
    n = x.shape[-1]
    m = jnp.max(x, axis=-1, keepdims=True)
    idx = jnp.min(jnp.where(x == m, jnp.arange(n, dtype=jnp.int32), n), axis=-1, keepdims=True)
    return m, idx


def _route(logits, bg, be):
    n = logits.shape[0]
    pg = jax.nn.softmax(logits[:, :N_GROUPS] + bg, axis=-1)
    pg_top, g_top = _top1(pg)
    el = (logits[:, N_GROUPS:N_GROUPS + N_EXPERTS] + be).reshape(n, N_GROUPS, EXPERTS_PER_GROUP)
    sel = g_top[:, :, None] == jnp.arange(N_GROUPS, dtype=jnp.int32)[None, :, None]
    el_sel = jnp.sum(jnp.where(sel, el, 0.0), axis=1)
    pe = jax.nn.softmax(el_sel, axis=-1)
    p1, e1 = _top1(pe)
    lane = jnp.arange(EXPERTS_PER_GROUP, dtype=jnp.int32)[None, :]
    p2, e2 = _top1(jnp.where(lane == e1, -jnp.inf, pe))
    pe_top = jnp.concatenate([p1, p2], axis=-1)
    e_top = jnp.concatenate([e1, e2], axis=-1)
    gates = pg_top * pe_top / jnp.sum(pe_top, axis=-1, keepdims=True)
    return g_top * EXPERTS_PER_GROUP + e_top, gates


def _dispatch(expert_idx):
    n = expert_idx.shape[0]
    a = n * TOP_K
    flat_e = expert_idx.reshape(-1).astype(jnp.int32)
    onehot = (flat_e[:, None] == jnp.arange(N_EXPERTS, dtype=jnp.int32)[None, :]).astype(jnp.int32)
    seen = jnp.cumsum(onehot, axis=0)
    counts = seen[-1]
    padded = (counts + MOE_ROWS - 1) // MOE_ROWS * MOE_ROWS
    pend = jnp.cumsum(padded)
    pstart = pend - padded
    n_blocks = -(-(a + N_EXPERTS * (MOE_ROWS - 1)) // MOE_ROWS)
    blk_start = jnp.arange(n_blocks, dtype=jnp.int32) * MOE_ROWS
    blk_exp = jnp.minimum(jnp.searchsorted(pend, blk_start, side='right'), N_EXPERTS - 1).astype(jnp.int32)
    pos = jnp.sum(onehot * (seen - 1 + pstart[None, :]), axis=1)
    slot_tok = jnp.zeros((n_blocks * MOE_ROWS,), jnp.int32).at[pos].set(
        jnp.arange(a, dtype=jnp.int32) // TOP_K, mode="promise_in_bounds", unique_indices=True)
    return slot_tok, blk_exp, pos.reshape(n, TOP_K)


def kernel(x_prompt, x_sample, state_hgrn, cache_k, cache_v, p_prompt, p_sample, norm_mix, norm_ffn, norm_ple, norm_final, a_w_in, a_w_out, a_lb_logits, a_gnorm, b_w_in, b_w_out, b_lambda_q1, b_lambda_k1, b_lambda_q2, b_lambda_k2, b_subln, router_group_w, router_group_b, router_expert_w, router_expert_b, expert_w_gate, expert_w_up, expert_w_down, ple_w_up, ple_w_gate):
    bp, tp, d = x_prompt.shape
    bs, ts, _ = x_sample.shape
    n_p, n_s = bp * tp, bs * ts
    depth = norm_mix.shape[0]
    nh_b = d // (2 * HEAD)

    h = jnp.concatenate([x_prompt.reshape(n_p, d), x_sample.reshape(n_s, d)], axis=0)
    lb_all = jnp.cumsum(jax.nn.softmax(a_lb_logits.astype(F32), axis=0), axis=0)
    states_p, states_s, k_p, v_p, k_s, v_s = [], [], [], [], [], []

    for i in range(depth):
        j = i // 2
        xn = _rmsnorm(h, norm_mix[i], BF16)
        if i % 2 == 0:
            proj = _matmul(xn, a_w_in[j].astype(BF16))
            o_p, s_p = _hgrn2(proj, 0, bp, tp, lb_all[j], a_gnorm[j], None)
            o_s, s_s = _hgrn2(proj, n_p, bs, ts, lb_all[j], a_gnorm[j], state_hgrn[j])
            states_p.append(s_p)
            states_s.append(s_s)
            w_out = a_w_out[j]
        else:
            w_in = b_w_in[j]
            wk, wv = w_in[:, d:2 * d].astype(BF16), w_in[:, 2 * d:].astype(BF16)
            q = _matmul(xn, w_in[:, :d].astype(BF16), out_dtype=BF16, scale=HEAD ** -0.5 * LOG2E)
            (kp, kp16), vp = _matmul_rowsplit(xn, wk, rows=n_p), _matmul(xn, wv, rows=n_p)
            (ks, ks16), vs = _matmul_rowsplit(xn, wk, row0=n_p, rows=n_s), _matmul(xn, wv, row0=n_p, rows=n_s)
            lam0 = 0.8 - 0.6 * math.exp(-0.3 * i)
            lam = (jnp.exp(jnp.sum(b_lambda_q1[j] * b_lambda_k1[j]))
                   - jnp.exp(jnp.sum(b_lambda_q2[j] * b_lambda_k2[j])) + lam0).astype(F32)
            o_p = _attn_prompt(q, kp16, vp, lam, b_subln[j], 1.0 - lam0, bp, tp)
            o_s = _attn_sample(q, n_p, ks16, vs, cache_k[j], cache_v[j], lam, b_subln[j], 1.0 - lam0, bs, ts)
            k_p.append(kp.reshape(bp, tp, nh_b, 2, HEAD))
            v_p.append(vp.reshape(bp, tp, nh_b, 2 * HEAD))
            k_s.append(ks.reshape(bs, ts, nh_b, 2, HEAD))
            v_s.append(vs.reshape(bs, ts, nh_b, 2 * HEAD))
            w_out = b_w_out[j]
        h = _matmul_residual2(o_p, o_s, w_out.astype(BF16), h)

        w_router = jnp.zeros((d, ROUTER_LANES), F32)
        w_router = w_router.at[:, :N_GROUPS].set(router_group_w[i])
        w_router = w_router.at[:, N_GROUPS:N_GROUPS + N_EXPERTS].set(router_expert_w[i])
        xq, logits = _rmsnorm_router(h, norm_ffn[i], w_router)
        expert_idx, gates = _route(logits, router_group_b[i], router_expert_b[i])
        slot_tok, blk_exp, pos = _dispatch(expert_idx)
        yb = _moe_experts(_moe_dispatch(xq, slot_tok), blk_exp, i, expert_w_gate, expert_w_up, expert_w_down)
        h = _moe_combine(h, gates, yb, pos)

        p_i = jnp.concatenate([p_prompt[i].reshape(n_p, -1), p_sample[i].reshape(n_s, -1)], axis=0)
        xn = _rmsnorm(h, norm_ple[i], BF16)
        h = _matmul_ple(xn, ple_w_gate[i].astype(BF16), h, p_i.astype(BF16), ple_w_up[i].astype(BF16))

    y_p = _rmsnorm(h, norm_final, F32, 0, n_p)
    y_s = _rmsnorm(h, norm_final, F32, n_p, n_s)
    return (y_p.reshape(bp, tp, d), y_s.reshape(bs, ts, d), jnp.stack(states_p), jnp.stack(states_s),
            jnp.stack(k_p), jnp.stack(v_p), jnp.stack(k_s), jnp.stack(v_s))
```

```python
import functools
import math

import jax
import jax.numpy as jnp
import numpy as np
from jax import lax
from jax.experimental import pallas as pl
from jax.experimental.pallas import tpu as pltpu

F32 = jnp.float32
BF16 = jnp.bfloat16

EPS = 1e-6
HEAD = 128
STREAM_CHUNK = 64
N_GROUPS = 4
EXPERTS_PER_GROUP = 8
N_EXPERTS = N_GROUPS * EXPERTS_PER_GROUP
TOP_K = 2
ROUTER_LANES = 128
MOE_ROWS = 256
RANK_GROUP = 128
HGRN2_HEADS_PER_STEP = 8
DMA_ISSUE_UNROLL = 8
ATTN_Q_ROWS = 512
V7X_VMEM_LIMIT = 56 * 1024 * 1024
LOG2E = math.log2(math.e)


def _tile(dim, want):
    t = min(dim, want)
    while dim % t:
        t //= 2
    return t


def _params(sem):
    return pltpu.CompilerParams(dimension_semantics=sem, vmem_limit_bytes=V7X_VMEM_LIMIT)


def _split3(x):
    hi = x.astype(BF16)
    r1 = x - hi.astype(F32)
    mid = r1.astype(BF16)
    lo = (r1 - mid.astype(F32)).astype(BF16)
    return hi, mid, lo


def _dot(a, b):
    return jnp.dot(a, b, preferred_element_type=F32)


def _dot_nt(a, b):
    return lax.dot_general(a, b, (((1,), (1,)), ((), ())), preferred_element_type=F32)


def _rmsnorm_kernel(x_ref, g_ref, o_ref):
    x = x_ref[...]
    y = x * lax.rsqrt(jnp.mean(x * x, axis=-1, keepdims=True) + EPS)
    o_ref[...] = (y * g_ref[...]).astype(o_ref.dtype)


def _rmsnorm(x, g, out_dtype, row0=0, rows=None):
    n, d = x.shape
    rows = n if rows is None else rows
    tm = _tile(math.gcd(rows, row0) if row0 else rows, 512)
    blk0 = row0 // tm
    return pl.pallas_call(
        _rmsnorm_kernel,
        out_shape=jax.ShapeDtypeStruct((rows, d), out_dtype),
        grid=(rows // tm,),
        in_specs=[pl.BlockSpec((tm, d), lambda i: (blk0 + i, 0)), pl.BlockSpec((1, d), lambda i: (0, 0))],
        out_specs=pl.BlockSpec((tm, d), lambda i: (i, 0)),
        compiler_params=_params(("parallel",)),
        name="rmsnorm",
    )(x, g.reshape(1, d))


def _rmsnorm2_kernel(xa_ref, xb_ref, g_ref, o_ref, *, a_tiles):
    i = pl.program_id(0)

    @pl.when(i < a_tiles)
    def _():
        _rmsnorm_kernel(xa_ref, g_ref, o_ref)

    @pl.when(i >= a_tiles)
    def _():
        _rmsnorm_kernel(xb_ref, g_ref, o_ref)


def _rmsnorm2(xa, xb, g, out_dtype):
    na, d = xa.shape
    nb = xb.shape[0]
    tm = _tile(math.gcd(na, nb), 512)
    a_tiles = na // tm
    return pl.pallas_call(
        functools.partial(_rmsnorm2_kernel, a_tiles=a_tiles),
        out_shape=jax.ShapeDtypeStruct((na + nb, d), out_dtype),
        grid=((na + nb) // tm,),
        in_specs=[pl.BlockSpec((tm, d), lambda i: (jnp.minimum(i, a_tiles - 1), 0)),
                  pl.BlockSpec((tm, d), lambda i: (jnp.maximum(i - a_tiles, 0), 0)),
                  pl.BlockSpec((1, d), lambda i: (0, 0))],
        out_specs=pl.BlockSpec((tm, d), lambda i: (i, 0)),
        compiler_params=_params(("parallel",)),
        name="rmsnorm",
    )(xa, xb, g.reshape(1, d))


def _rmsnorm_router_kernel(x_ref, g_ref, w0_ref, w1_ref, w2_ref, o_ref, l_ref):
    x = x_ref[...]
    y = x * lax.rsqrt(jnp.mean(x * x, axis=-1, keepdims=True) + EPS)
    xn = y * g_ref[...]
    o_ref[...] = _pack_halves(xn)
    x0, x1, x2 = _split3(xn)
    w0, w1, w2 = w0_ref[...], w1_ref[...], w2_ref[...]
    small = _dot(x0, w2) + _dot(x1, w1) + _dot(x2, w0)
    mid = _dot(x0, w1) + _dot(x1, w0)
    l_ref[...] = _dot(x0, w0) + (mid + small)


def _rmsnorm_router(x, g, w_router):
    n, d = x.shape
    tm = _tile(n, 512)
    w0, w1, w2 = _split3(w_router)
    wspec = pl.BlockSpec((d, ROUTER_LANES), lambda i: (0, 0))
    return pl.pallas_call(
        _rmsnorm_router_kernel,
        out_shape=(jax.ShapeDtypeStruct((n, d // 2), jnp.uint32), jax.ShapeDtypeStruct((n, ROUTER_LANES), F32)),
        grid=(n // tm,),
        in_specs=[pl.BlockSpec((tm, d), lambda i: (i, 0)), pl.BlockSpec((1, d), lambda i: (0, 0)),
                  wspec, wspec, wspec],
        out_specs=(pl.BlockSpec((tm, d // 2), lambda i: (i, 0)), pl.BlockSpec((tm, ROUTER_LANES), lambda i: (i, 0))),
        compiler_params=_params(("parallel",)),
        name="rmsnorm_router",
    )(x, g.reshape(1, d), w0, w1, w2)


def _mm_kernel(x_ref, w_ref, o_ref, *, scale):
    acc = _dot(x_ref[...], w_ref[...])
    if scale is not None:
        acc = acc * scale
    o_ref[...] = acc.astype(o_ref.dtype)


def _mm_rowsplit_kernel(x_ref, w_ref, o_ref, o16_ref):
    acc = _dot(x_ref[...], w_ref[...])
    o16_ref[...] = acc.astype(o16_ref.dtype)
    for c in range(o_ref.shape[1]):
        o_ref[:, c, :] = acc[:, c * HEAD:(c + 1) * HEAD]


def _matmul_rowsplit(x, w, *, row0=0, rows=None, tm=1024, tn=1024):
    m, k = x.shape
    rows = m if rows is None else rows
    n = w.shape[1]
    tm = _tile(math.gcd(rows, row0) if row0 else rows, tm)
    tn = _tile(n, tn)
    blk0 = row0 // tm
    return pl.pallas_call(
        _mm_rowsplit_kernel,
        out_shape=(jax.ShapeDtypeStruct((rows, n // HEAD, HEAD), F32), jax.ShapeDtypeStruct((rows, n), BF16)),
        grid=(rows // tm, n // tn),
        in_specs=[pl.BlockSpec((tm, k), lambda i, j: (blk0 + i, 0)), pl.BlockSpec((k, tn), lambda i, j: (0, j))],
        out_specs=(pl.BlockSpec((tm, tn // HEAD, HEAD), lambda i, j: (i, j, 0)),
                   pl.BlockSpec((tm, tn), lambda i, j: (i, j))),
        compiler_params=_params(("parallel", "arbitrary")),
        name="matmul_rowsplit",
    )(x, w)


def _mm_res2_kernel(xa_ref, xb_ref, w_ref, ra_ref, rb_ref, o_ref, *, a_tiles):
    i = pl.program_id(0)

    @pl.when(i < a_tiles)
    def _():
        o_ref[...] = ra_ref[...] + _dot(xa_ref[...], w_ref[...])

    @pl.when(i >= a_tiles)
    def _():
        o_ref[...] = rb_ref[...] + _dot(xb_ref[...], w_ref[...])


def _mm_ple_kernel(x_ref, w_ref, r_ref, p_ref, wup_ref, o_ref):
    gate = jax.nn.sigmoid(_dot(x_ref[...], w_ref[...]))
    o_ref[...] = r_ref[...] + _dot(p_ref[...], wup_ref[...]) * gate


def _matmul(x, w, *, out_dtype=F32, scale=None, row0=0, rows=None, tm=1024, tn=1024):
    m, k = x.shape
    rows = m if rows is None else rows
    n = w.shape[1]
    tm = _tile(math.gcd(rows, row0) if row0 else rows, tm)
    tn = _tile(n, tn)
    blk0 = row0 // tm
    return pl.pallas_call(
        functools.partial(_mm_kernel, scale=scale),
        out_shape=jax.ShapeDtypeStruct((rows, n), out_dtype),
        grid=(rows // tm, n // tn),
        in_specs=[pl.BlockSpec((tm, k), lambda i, j: (blk0 + i, 0)), pl.BlockSpec((k, tn), lambda i, j: (0, j))],
        out_specs=pl.BlockSpec((tm, tn), lambda i, j: (i, j)),
        compiler_params=_params(("parallel", "arbitrary")),
        name="matmul",
    )(x, w)


def _matmul_residual2(xa, xb, w, res):
    ma, k = xa.shape
    mb = xb.shape[0]
    n = w.shape[1]
    tm, tn = _tile(math.gcd(ma, mb), 1024), _tile(n, 512)
    a_tiles, b_tiles = ma // tm, mb // tm
    in_a = lambda i: jnp.minimum(i, a_tiles - 1)
    in_b = lambda i: jnp.maximum(i - a_tiles, 0)
    if isinstance(res, tuple):
        b_off = 0
    else:
        res, b_off = (res, res), a_tiles
    return pl.pallas_call(
        functools.partial(_mm_res2_kernel, a_tiles=a_tiles),
        out_shape=jax.ShapeDtypeStruct((ma + mb, n), F32),
        grid=(a_tiles + b_tiles, n // tn),
        in_specs=[pl.BlockSpec((tm, k), lambda i, j: (in_a(i), 0)),
                  pl.BlockSpec((tm, k), lambda i, j: (in_b(i), 0)),
                  pl.BlockSpec((k, tn), lambda i, j: (0, j)),
                  pl.BlockSpec((tm, tn), lambda i, j: (in_a(i), jnp.where(i < a_tiles, j, n // tn - 1))),
                  pl.BlockSpec((tm, tn), lambda i, j: (b_off + in_b(i), jnp.where(i < a_tiles, 0, j)))],
        out_specs=pl.BlockSpec((tm, tn), lambda i, j: (i, j)),
        compiler_params=_params(("parallel", "arbitrary")),
        name="matmul_residual",
    )(xa, xb, w, *res)


def _matmul_ple(xn, w_gate, res, p, w_up, *, tm=1024, tn=512):
    m, k = xn.shape
    n = w_gate.shape[1]
    kp = p.shape[1]
    tm, tn = _tile(m, tm), _tile(n, tn)
    return pl.pallas_call(
        _mm_ple_kernel,
        out_shape=jax.ShapeDtypeStruct((m, n), F32),
        grid=(m // tm, n // tn),
        in_specs=[pl.BlockSpec((tm, k), lambda i, j: (i, 0)), pl.BlockSpec((k, tn), lambda i, j: (0, j)),
                  pl.BlockSpec((tm, tn), lambda i, j: (i, j)),
                  pl.BlockSpec((tm, kp), lambda i, j: (i, 0)), pl.BlockSpec((kp, tn), lambda i, j: (0, j))],
        out_specs=pl.BlockSpec((tm, tn), lambda i, j: (i, j)),
        compiler_params=_params(("parallel", "arbitrary")),
        name="matmul_ple",
    )(xn, w_gate, res, p, w_up)


def _hgrn2_level_table(c):
    t = np.arange(c)[:, None]
    s = np.arange(c)[None, :]
    x = np.bitwise_xor(t, s)
    lvl = np.where(x > 0, np.floor(np.log2(np.maximum(x, 1))).astype(np.int32), -1)
    return np.where(s < t, lvl, -1).astype(np.int32)


def _hgrn2_sign_table(c):
    t = np.arange(c)[None, :, None]
    j = np.arange(int(math.log2(c)))[:, None, None]
    return np.broadcast_to(np.where((t >> j) & 1, 1.0, -1.0), (j.shape[0], c, HEAD)).astype(np.float32)


def _hgrn2_kernel(q_ref, f_ref, v_ref, g_ref, lb_ref, gn_ref, lvl_ref, tri_ref, sgn_ref, *rest, chunk, n_chunks, hp,
                  has_s0):
    if has_s0:
        s0_ref, o_ref, s_ref, st_ref, b_ref = rest
    else:
        o_ref, s_ref, st_ref, b_ref = rest
    tb = pl.program_id(2)
    c = chunk

    @pl.when(tb == 0)
    def _():
        for hh in range(hp):
            st_ref[hh] = s0_ref[0, hh].T if has_s0 else jnp.zeros((HEAD, HEAD), F32)

    gn = gn_ref[...]
    lvl = lvl_ref[...]
    tri = tri_ref[...]
    sub = lax.broadcasted_iota(jnp.int32, (8, HEAD), 0)
    n_levels = int(math.log2(c))

    def gates(rows, hh):
        lanes = slice(hh * HEAD, (hh + 1) * HEAD)
        lb = lb_ref[:, lanes]
        q = jax.nn.silu(q_ref[rows, lanes])
        f = lb + (1.0 - lb) * jax.nn.sigmoid(f_ref[rows, lanes])
        k = 1.0 - f
        lg = jnp.log(f) * LOG2E
        l0, l1, l2 = _split3(lg)
        cs = _dot(tri, jnp.concatenate([l0, l1, l2], axis=1))
        b = cs[:, :HEAD] + (cs[:, HEAD:2 * HEAD] + cs[:, 2 * HEAD:])
        b_ref[hh] = b
        att = jnp.where(lvl == 0, _dot_nt((q * f).astype(BF16), k.astype(BF16)), 0.0)
        return q, k, b, att

    def level(j, hh, q, k, b, att):
        m = 1 << j
        pieces = []
        for g8 in range(c // 8):
            if m >= 4:
                r = (g8 * 8 // (2 * m)) * 2 * m + m - 1
                pieces.append(jnp.broadcast_to(b_ref[hh, r:r + 1, :], (8, HEAD)))
            else:
                top = jnp.broadcast_to(b_ref[hh, g8 * 8 + 1:g8 * 8 + 2, :], (8, HEAD))
                bot = jnp.broadcast_to(b_ref[hh, g8 * 8 + 5:g8 * 8 + 6, :], (8, HEAD))
                pieces.append(jnp.where(sub < 4, top, bot))
        e = jnp.exp2((b - jnp.concatenate(pieces, axis=0)) * sgn_ref[j])
        sj = _dot_nt((q * e).astype(BF16), (k * e).astype(BF16))
        return jnp.where(lvl == j, sj, att)

    def finish(rows, hh, q, k, b, att):
        lanes = slice(hh * HEAD, (hh + 1) * HEAD)
        v = v_ref[rows, lanes]
        st = st_ref[hh]
        b_end = b_ref[hh, c - 1:c, :]
        o = _dot(att.astype(BF16), v.astype(BF16))
        o = o + _dot_nt((q * jnp.exp2(b)).astype(BF16), st.astype(BF16))
        o = o + jnp.sum(q * k, axis=-1, keepdims=True) * v
        kd = (k * jnp.exp2(b_end - b)).astype(BF16)
        st_ref[hh] = jnp.exp2(b_end) * st + _dot(v.T.astype(BF16), kd)
        o = o * lax.rsqrt(jnp.mean(o * o, axis=-1, keepdims=True) + EPS) * gn
        o_ref[rows, lanes] = (o * jax.nn.silu(g_ref[rows, lanes])).astype(o_ref.dtype)

    def one_chunk(ci, carry):
        rows = pl.ds(pl.multiple_of(ci * c, c), c)
        work = [gates(rows, hh) for hh in range(hp)]
        for j in range(1, n_levels):
            work = [w[:3] + (level(j, hh, *w),) for hh, w in enumerate(work)]
        for hh, w in enumerate(work):
            finish(rows, hh, *w)
        return carry

    lax.fori_loop(0, n_chunks, one_chunk, 0)

    @pl.when(tb == pl.num_programs(2) - 1)
    def _():
        for hh in range(hp):
            s_ref[0, hh] = st_ref[hh].T


def _hgrn2(proj, row_off, bsz, t, lb, gnorm, s0):
    d = proj.shape[1] // 4
    nh = d // HEAD
    hp = _tile(nh, HGRN2_HEADS_PER_STEP)
    c = min(t, 128)
    n_levels = int(math.log2(c))
    tb = _tile(t, 512)
    n_tb = t // tb
    assert row_off % tb == 0 and tb % c == 0
    blk0 = row_off // tb
    ng = nh // hp
    col = lambda kind: pl.BlockSpec((tb, hp * HEAD), lambda b, h, i: (blk0 + b * n_tb + i, kind * ng + h))
    const = lambda shape: pl.BlockSpec(shape, lambda b, h, i: (0, 0))
    st_spec = pl.BlockSpec((1, hp, HEAD, HEAD), lambda b, h, i: (b, h, 0, 0))
    in_specs = [col(0), col(1), col(2), col(3), pl.BlockSpec((1, hp * HEAD), lambda b, h, i: (0, h)),
                const((1, HEAD)), const((c, c)), const((c, c)),
                pl.BlockSpec((n_levels, c, HEAD), lambda b, h, i: (0, 0, 0))]
    args = [proj, proj, proj, proj, lb.reshape(1, d), gnorm.reshape(1, HEAD),
            jnp.asarray(_hgrn2_level_table(c)), jnp.asarray(np.tril(np.ones((c, c), np.float32)), BF16),
            jnp.asarray(_hgrn2_sign_table(c))]
    if s0 is not None:
        in_specs.append(st_spec)
        args.append(s0)
    return pl.pallas_call(
        functools.partial(_hgrn2_kernel, chunk=c, n_chunks=tb // c, hp=hp, has_s0=s0 is not None),
        out_shape=(jax.ShapeDtypeStruct((bsz * t, d), BF16), jax.ShapeDtypeStruct((bsz, nh, HEAD, HEAD), F32)),
        grid=(bsz, ng, n_tb),
        in_specs=in_specs,
        out_specs=(pl.BlockSpec((tb, hp * HEAD), lambda b, h, i: (b * n_tb + i, h)), st_spec),
        scratch_shapes=[pltpu.VMEM((hp, HEAD, HEAD), F32), pltpu.VMEM((hp, c, HEAD), F32)],
        compiler_params=_params(("parallel", "parallel", "arbitrary")),
        name="hgrn2_scan",
    )(*args)


def _subln(o, g, post_scale):
    return o * lax.rsqrt(jnp.mean(o * o, axis=-1, keepdims=True) + EPS) * g * post_scale


def _attn_prompt_kernel(lam_ref, q_ref, kb_ref, v_ref, g_ref, o_ref, vb_ref, *, tq, n_q, post_scale):
    qi = pl.program_id(2)

    @pl.when(qi == 0)
    def _():
        vb_ref[...] = v_ref[...].astype(BF16)

    lam = lam_ref[0]
    q = q_ref[...]
    for vi in range(n_q):
        @pl.when(qi == vi)
        def _(head=vi * tq, ext=(vi + 1) * tq):
            pos = lax.broadcasted_iota(jnp.int32, (tq, tq), 0)
            allowed = lax.broadcasted_iota(jnp.int32, (tq, tq), 1) <= (pos | (STREAM_CHUNK - 1))
            v_tail = vb_ref[head:ext, :]

            def one_map(c0):
                qc = q[:, c0:c0 + HEAD]
                st = jnp.where(allowed, _dot_nt(qc, kb_ref[head:ext, c0:c0 + HEAD]), -jnp.inf)
                m = jnp.max(st, axis=-1, keepdims=True)
                if head:
                    sh = _dot_nt(qc, kb_ref[0:head, c0:c0 + HEAD])
                    m = jnp.maximum(m, jnp.max(sh, axis=-1, keepdims=True))
                et = jnp.exp2(st - m)
                tot = jnp.sum(et, axis=-1, keepdims=True)
                acc = _dot(et.astype(BF16), v_tail)
                if head:
                    eh = jnp.exp2(sh - m)
                    tot = tot + jnp.sum(eh, axis=-1, keepdims=True)
                    acc = acc + _dot(eh.astype(BF16), vb_ref[0:head, :])
                return acc / tot

            o = one_map(0) - lam * one_map(HEAD)
            o_ref[...] = _subln(o, g_ref[...], post_scale).astype(o_ref.dtype)


def _attn_prompt(q, k, v, lam, subln, post_scale, bsz, t):
    d = q.shape[1]
    hw = 2 * HEAD
    nh = d // hw
    tq = _tile(t, ATTN_Q_ROWS)
    n_q = t // tq
    return pl.pallas_call(
        functools.partial(_attn_prompt_kernel, tq=tq, n_q=n_q, post_scale=post_scale),
        out_shape=jax.ShapeDtypeStruct((bsz * t, d), BF16),
        grid=(bsz, nh, n_q),
        in_specs=[pl.BlockSpec(memory_space=pltpu.SMEM),
                  pl.BlockSpec((tq, hw), lambda b, h, i: (b * n_q + i, h)),
                  pl.BlockSpec((t, hw), lambda b, h, i: (b, h)),
                  pl.BlockSpec((t, hw), lambda b, h, i: (b, h)),
                  pl.BlockSpec((1, hw), lambda b, h, i: (0, 0))],
        out_specs=pl.BlockSpec((tq, hw), lambda b, h, i: (b * n_q + i, h)),
        scratch_shapes=[pltpu.VMEM((t, hw), BF16)],
        compiler_params=_params(("parallel", "parallel", "arbitrary")),
        name="diff_attn_prompt",
    )(lam.reshape(1), q, k, v, subln.reshape(1, hw))


def _attn_sample_kernel(lam_ref, q_ref, kn_ref, vn_ref, ck_ref, cv_ref, g_ref, o_ref, *, hg, post_scale):
    hw = 2 * HEAD
    lam = lam_ref[0]
    past = ck_ref.shape[0]
    ck_rows = ck_ref.reshape(past * 2 * hg, HEAD)
    for h in range(hg):
        lanes = slice(h * hw, (h + 1) * hw)
        v = cv_ref[:, lanes].astype(BF16)
        vn = vn_ref[:, lanes].astype(BF16)
        outs = []
        for c in range(2):
            col = slice((2 * h + c) * HEAD, (2 * h + c + 1) * HEAD)
            qc = q_ref[:, col]
            s = _dot_nt(qc, ck_rows[pl.ds(2 * h + c, past, stride=2 * hg), :].astype(BF16))
            sn = _dot_nt(qc, kn_ref[:, col])
            m = jnp.maximum(jnp.max(s, axis=-1, keepdims=True), jnp.max(sn, axis=-1, keepdims=True))
            e, en = jnp.exp2(s - m), jnp.exp2(sn - m)
            tot = jnp.sum(e, axis=-1, keepdims=True) + jnp.sum(en, axis=-1, keepdims=True)
            outs.append((_dot(e.astype(BF16), v) + _dot(en.astype(BF16), vn)) / tot)
        o = outs[0] - lam * outs[1]
        o_ref[:, lanes] = _subln(o, g_ref[...], post_scale).astype(o_ref.dtype)


def _attn_sample(q, q_row0, k, v, cache_k, cache_v, lam, subln, post_scale, bsz, t):
    d = q.shape[1]
    hw = 2 * HEAD
    nh = d // hw
    past = cache_k.shape[1]
    hg = _tile(nh, 4)
    assert (2 * hg) % 8 == 0 or hg == nh
    assert q_row0 % t == 0
    blk0 = q_row0 // t
    new = pl.BlockSpec((t, hg * hw), lambda b, g: (b, g))
    return pl.pallas_call(
        functools.partial(_attn_sample_kernel, hg=hg, post_scale=post_scale),
        out_shape=jax.ShapeDtypeStruct((bsz * t, d), BF16),
        grid=(bsz, nh // hg),
        in_specs=[pl.BlockSpec(memory_space=pltpu.SMEM),
                  pl.BlockSpec((t, hg * hw), lambda b, g: (blk0 + b, g)), new, new,
                  pl.BlockSpec((past, 2 * hg, HEAD), lambda b, g: (b, g, 0)),
                  pl.BlockSpec((past, hg * hw), lambda b, g: (b, g)),
                  pl.BlockSpec((1, hw), lambda b, g: (0, 0))],
        out_specs=pl.BlockSpec((t, hg * hw), lambda b, g: (b, g)),
        compiler_params=_params(("parallel", "parallel")),
        name="diff_attn_sample",
    )(lam.reshape(1), q, k, v, cache_k.reshape(bsz * past, 2 * nh, HEAD), cache_v.reshape(bsz * past, d),
      subln.reshape(1, hw))


def _expert_changed(be_ref):
    b = pl.program_id(1)
    return (b == 0) | (be_ref[b] != be_ref[jnp.maximum(b - 1, 0)])


def _pack_halves(x):
    half = x.shape[1] // 2
    as_bits = lambda v: lax.bitcast_convert_type(v.astype(BF16).astype(F32), jnp.uint32)
    return (as_bits(x[:, :half]) >> 16) | (as_bits(x[:, half:]) & jnp.uint32(0xFFFF0000))


def _unpack_halves(u):
    lo = lax.bitcast_convert_type(u << 16, F32).astype(BF16)
    hi = lax.bitcast_convert_type(u & jnp.uint32(0xFFFF0000), F32).astype(BF16)
    return lo, hi


def _moe_dispatch_kernel(tok_ref, nu_ref, x_hbm, o_ref, buf_ref, sem_ref, *, rows):
    i = pl.program_id(0)
    n_used = nu_ref[0]

    def copy(tile, r, slot):
        return pltpu.make_async_copy(x_hbm.at[tok_ref[tile * rows + r]], buf_ref.at[slot, r], sem_ref.at[slot])

    def start_tile(tile, slot):
        lax.fori_loop(0, rows, lambda r, c: (copy(tile, r, slot).start(), c)[1], 0, unroll=DMA_ISSUE_UNROLL)

    @pl.when(i == 0)
    def _():
        start_tile(0, 0)

    @pl.when(i + 1 < n_used)
    def _():
        start_tile(i + 1, (i + 1) % 2)

    @pl.when(i < n_used)
    def _():
        slot = i % 2
        lax.fori_loop(0, rows, lambda r, c: (copy(i, r, slot).wait(), c)[1], 0, unroll=DMA_ISSUE_UNROLL)
        o_ref[...] = buf_ref[slot]

    @pl.when(i >= n_used)
    def _():
        o_ref[...] = jnp.zeros(o_ref.shape, o_ref.dtype)


def _moe_dispatch(xq, slot_tok, n_used):
    w = xq.shape[1]
    p = slot_tok.shape[0]
    rows = MOE_ROWS
    return pl.pallas_call(
        functools.partial(_moe_dispatch_kernel, rows=rows),
        out_shape=jax.ShapeDtypeStruct((p, w), xq.dtype),
        grid_spec=pltpu.PrefetchScalarGridSpec(
            num_scalar_prefetch=2, grid=(p // rows,),
            in_specs=[pl.BlockSpec(memory_space=pl.ANY)],
            out_specs=pl.BlockSpec((rows, w), lambda i, tok, nu: (i, 0)),
            scratch_shapes=[pltpu.VMEM((2, rows, w), xq.dtype), pltpu.SemaphoreType.DMA((2,))]),
        compiler_params=_params(("arbitrary",)),
        name="moe_dispatch",
    )(slot_tok, n_used, xq)


def _moe_combine_kernel(pos_ref, h_ref, g_ref, yb_hbm, o_ref, buf_ref, sem_ref, *, rows):
    i = pl.program_id(0)
    n_tiles = pl.num_programs(0)

    def copy(tile, r, k, slot):
        return pltpu.make_async_copy(yb_hbm.at[pos_ref[(tile * rows + r) * TOP_K + k]], buf_ref.at[slot, k, r],
                                     sem_ref.at[slot])

    def start_tile(tile, slot):
        def body(r, carry):
            for k in range(TOP_K):
                copy(tile, r, k, slot).start()
            return carry
        lax.fori_loop(0, rows, body, 0, unroll=DMA_ISSUE_UNROLL)

    @pl.when(i == 0)
    def _():
        start_tile(0, 0)

    @pl.when(i + 1 < n_tiles)
    def _():
        start_tile(i + 1, (i + 1) % 2)

    slot = i % 2

    def drain(r, carry):
        for k in range(TOP_K):
            copy(i, r, k, slot).wait()
        return carry

    lax.fori_loop(0, rows, drain, 0, unroll=DMA_ISSUE_UNROLL)
    g = g_ref[...]
    y = buf_ref[slot, 0] * g[:, 0:1]
    for k in range(1, TOP_K):
        y = y + buf_ref[slot, k] * g[:, k:k + 1]
    o_ref[...] = h_ref[...] + y


def _moe_combine(h, gates, yb, pos):
    n, d = h.shape
    rows = _tile(n, 128)
    return pl.pallas_call(
        functools.partial(_moe_combine_kernel, rows=rows),
        out_shape=jax.ShapeDtypeStruct((n, d), F32),
        grid_spec=pltpu.PrefetchScalarGridSpec(
            num_scalar_prefetch=1, grid=(n // rows,),
            in_specs=[pl.BlockSpec((rows, d), lambda i, pos: (i, 0)),
                      pl.BlockSpec((rows, TOP_K), lambda i, pos: (i, 0)),
                      pl.BlockSpec(memory_space=pl.ANY)],
            out_specs=pl.BlockSpec((rows, d), lambda i, pos: (i, 0)),
            scratch_shapes=[pltpu.VMEM((2, TOP_K, rows, d), F32), pltpu.SemaphoreType.DMA((2,))]),
        compiler_params=_params(("arbitrary",)),
        name="moe_combine",
    )(pos.reshape(-1), h, gates, yb)


def _moe_up_kernel(be_ref, we_ref, wj_ref, nu_ref, x_ref, wg_ref, wu_ref, o_ref, wgb_ref, wub_ref):
    del we_ref, wj_ref

    @pl.when(pl.program_id(1) < nu_ref[0])
    def _():
        @pl.when(_expert_changed(be_ref))
        def _():
            wgb_ref[...] = wg_ref[...].astype(BF16)
            wub_ref[...] = wu_ref[...].astype(BF16)

        xl, xh = _unpack_halves(x_ref[...])
        half = xl.shape[1]
        g = _dot(xl, wgb_ref[0:half, :]) + _dot(xh, wgb_ref[half:, :])
        u = _dot(xl, wub_ref[0:half, :]) + _dot(xh, wub_ref[half:, :])
        o_ref[...] = (jax.nn.silu(g) * u).astype(o_ref.dtype)

    @pl.when(pl.program_id(1) >= nu_ref[0])
    def _():
        o_ref[...] = jnp.zeros(o_ref.shape, o_ref.dtype)


def _moe_down_kernel(be_ref, we_ref, wj_ref, nu_ref, h_ref, wd_ref, o_ref, wdb_ref):
    del we_ref, wj_ref

    @pl.when(pl.program_id(1) < nu_ref[0])
    def _():
        @pl.when(_expert_changed(be_ref))
        def _():
            wdb_ref[...] = wd_ref[...].astype(BF16)

        o_ref[...] = _dot(h_ref[...], wdb_ref[...])

    @pl.when(pl.program_id(1) >= nu_ref[0])
    def _():
        o_ref[...] = jnp.zeros(o_ref.shape, o_ref.dtype)


def _moe_experts(xs, plan, layer, w_gate, w_up, w_down):
    blk_exp, w_exp, w_joff, n_used = plan
    p, w = xs.shape
    d = 2 * w
    de = w_gate.shape[3]
    nb = p // MOE_ROWS
    te = _tile(de, 512)
    row = lambda j, b, be, we, wj, nu: (jnp.minimum(b, nu[0] - 1), 0)
    out = lambda j, b, be, we, wj, nu: (b, j)

    def weight(n_j):
        return lambda j, b, be, we, wj, nu: (layer, we[b], 0, jnp.minimum(j + wj[b], n_j - 1))

    hidden = pl.pallas_call(
        _moe_up_kernel,
        out_shape=jax.ShapeDtypeStruct((p, de), BF16),
        grid_spec=pltpu.PrefetchScalarGridSpec(
            num_scalar_prefetch=4, grid=(de // te, nb),
            in_specs=[pl.BlockSpec((MOE_ROWS, w), row),
                      pl.BlockSpec((None, None, d, te), weight(de // te)),
                      pl.BlockSpec((None, None, d, te), weight(de // te))],
            out_specs=pl.BlockSpec((MOE_ROWS, te), out),
            scratch_shapes=[pltpu.VMEM((d, te), BF16), pltpu.VMEM((d, te), BF16)]),
        compiler_params=_params(("arbitrary", "arbitrary")),
        name="moe_gate_up",
    )(blk_exp, w_exp, w_joff, n_used, xs, w_gate, w_up)
    tn = _tile(d, 2048)
    return pl.pallas_call(
        _moe_down_kernel,
        out_shape=jax.ShapeDtypeStruct((p, d), F32),
        grid_spec=pltpu.PrefetchScalarGridSpec(
            num_scalar_prefetch=4, grid=(d // tn, nb),
            in_specs=[pl.BlockSpec((MOE_ROWS, de), row),
                      pl.BlockSpec((None, None, de, tn), weight(d // tn))],
            out_specs=pl.BlockSpec((MOE_ROWS, tn), out),
            scratch_shapes=[pltpu.VMEM((de, tn), BF16)]),
        compiler_params=_params(("arbitrary", "arbitrary")),
        name="moe_down",
    )(blk_exp, w_exp, w_joff, n_used, hidden, w_down)


def _top1(x):
    n = x.shape[-1]
    m = jnp.max(x, axis=-1, keepdims=True)
    idx = jnp.min(jnp.where(x == m, jnp.arange(n, dtype=jnp.int32), n), axis=-1, keepdims=True)
    return m, idx


def _route(logits, bg, be):
    n = logits.shape[0]
    pg = jax.nn.softmax(logits[:, :N_GROUPS] + bg, axis=-1)
    pg_top, g_top = _top1(pg)
    el = (logits[:, N_GROUPS:N_GROUPS + N_EXPERTS] + be).reshape(n, N_GROUPS, EXPERTS_PER_GROUP)
    sel = g_top[:, :, None] == jnp.arange(N_GROUPS, dtype=jnp.int32)[None, :, None]
    el_sel = jnp.sum(jnp.where(sel, el, 0.0), axis=1)
    pe = jax.nn.softmax(el_sel, axis=-1)
    p1, e1 = _top1(pe)
    lane = jnp.arange(EXPERTS_PER_GROUP, dtype=jnp.int32)[None, :]
    p2, e2 = _top1(jnp.where(lane == e1, -jnp.inf, pe))
    pe_top = jnp.concatenate([p1, p2], axis=-1)
    e_top = jnp.concatenate([e1, e2], axis=-1)
    gates = pg_top * pe_top / jnp.sum(pe_top, axis=-1, keepdims=True)
    return g_top * EXPERTS_PER_GROUP + e_top, gates


def _dispatch(expert_idx):
    n = expert_idx.shape[0]
    a = n * TOP_K
    flat_e = expert_idx.reshape(-1).astype(jnp.int32)
    onehot = (flat_e[:, None] == jnp.arange(N_EXPERTS, dtype=jnp.int32)[None, :]).astype(jnp.int32)
    grouped = onehot.reshape(a // RANK_GROUP, RANK_GROUP, N_EXPERTS)
    inner = jnp.cumsum(grouped, axis=1)
    totals = inner[:, -1, :]
    seen = (inner + (jnp.cumsum(totals, axis=0) - totals)[:, None, :]).reshape(a, N_EXPERTS)
    counts = seen[-1]
    padded = (counts + MOE_ROWS - 1) // MOE_ROWS * MOE_ROWS
    pend = jnp.cumsum(padded)
    pstart = pend - padded
    n_blocks = -(-(a + N_EXPERTS * (MOE_ROWS - 1)) // MOE_ROWS)
    pos = jnp.sum(onehot * (seen - 1 + pstart[None, :]), axis=1)
    slot_tok = jnp.zeros((n_blocks * MOE_ROWS,), jnp.int32).at[pos].set(
        jnp.arange(a, dtype=jnp.int32) // TOP_K, mode="promise_in_bounds", unique_indices=True)

    n_used = pend[-1] // MOE_ROWS
    blk = jnp.minimum(jnp.arange(n_blocks, dtype=jnp.int32), n_used - 1)
    expert_of = lambda b: jnp.minimum(jnp.searchsorted(pend, b * MOE_ROWS, side='right'), N_EXPERTS - 1).astype(jnp.int32)
    blk_exp = expert_of(blk)
    first = (blk == 0) | (blk_exp != expert_of(jnp.maximum(blk - 1, 0)))
    nxt = pend[blk_exp] // MOE_ROWS
    wraps = nxt >= n_used
    nxt_exp = jnp.where(wraps, expert_of(jnp.zeros_like(nxt)), expert_of(jnp.minimum(nxt, n_used - 1)))
    w_exp = jnp.where(first, blk_exp, nxt_exp)
    w_joff = jnp.where(first, 0, wraps.astype(jnp.int32))
    plan = (blk_exp, w_exp.astype(jnp.int32), w_joff.astype(jnp.int32), n_used.reshape(1).astype(jnp.int32))
    return slot_tok, pos.reshape(n, TOP_K), plan


def kernel(x_prompt, x_sample, state_hgrn, cache_k, cache_v, p_prompt, p_sample, norm_mix, norm_ffn, norm_ple, norm_final, a_w_in, a_w_out, a_lb_logits, a_gnorm, b_w_in, b_w_out, b_lambda_q1, b_lambda_k1, b_lambda_q2, b_lambda_k2, b_subln, router_group_w, router_group_b, router_expert_w, router_expert_b, expert_w_gate, expert_w_up, expert_w_down, ple_w_up, ple_w_gate):
    bp, tp, d = x_prompt.shape
    bs, ts, _ = x_sample.shape
    n_p, n_s = bp * tp, bs * ts
    depth = norm_mix.shape[0]
    nh_b = d // (2 * HEAD)

    h = (x_prompt.reshape(n_p, d), x_sample.reshape(n_s, d))
    lb_all = jnp.cumsum(jax.nn.softmax(a_lb_logits.astype(F32), axis=0), axis=0)
    states_p, states_s, k_p, v_p, k_s, v_s = [], [], [], [], [], []

    for i in range(depth):
        j = i // 2
        xn = _rmsnorm2(*h, norm_mix[i], BF16) if isinstance(h, tuple) else _rmsnorm(h, norm_mix[i], BF16)
        if i % 2 == 0:
            proj = _matmul(xn, a_w_in[j].astype(BF16))
            o_p, s_p = _hgrn2(proj, 0, bp, tp, lb_all[j], a_gnorm[j], None)
            o_s, s_s = _hgrn2(proj, n_p, bs, ts, lb_all[j], a_gnorm[j], state_hgrn[j])
            states_p.append(s_p)
            states_s.append(s_s)
            w_out = a_w_out[j]
        else:
            w_in = b_w_in[j]
            wk, wv = w_in[:, d:2 * d].astype(BF16), w_in[:, 2 * d:].astype(BF16)
            q = _matmul(xn, w_in[:, :d].astype(BF16), out_dtype=BF16, scale=HEAD ** -0.5 * LOG2E)
            (kp, kp16), vp = _matmul_rowsplit(xn, wk, rows=n_p), _matmul(xn, wv, rows=n_p)
            (ks, ks16), vs = _matmul_rowsplit(xn, wk, row0=n_p, rows=n_s), _matmul(xn, wv, row0=n_p, rows=n_s)
            lam0 = 0.8 - 0.6 * math.exp(-0.3 * i)
            lam = (jnp.exp(jnp.sum(b_lambda_q1[j] * b_lambda_k1[j]))
                   - jnp.exp(jnp.sum(b_lambda_q2[j] * b_lambda_k2[j])) + lam0).astype(F32)
            o_p = _attn_prompt(q, kp16, vp, lam, b_subln[j], 1.0 - lam0, bp, tp)
            o_s = _attn_sample(q, n_p, ks16, vs, cache_k[j], cache_v[j], lam, b_subln[j], 1.0 - lam0, bs, ts)
            k_p.append(kp.reshape(bp, tp, nh_b, 2, HEAD))
            v_p.append(vp.reshape(bp, tp, nh_b, 2 * HEAD))
            k_s.append(ks.reshape(bs, ts, nh_b, 2, HEAD))
            v_s.append(vs.reshape(bs, ts, nh_b, 2 * HEAD))
            w_out = b_w_out[j]
        h = _matmul_residual2(o_p, o_s, w_out.astype(BF16), h)

        w_router = jnp.zeros((d, ROUTER_LANES), F32)
        w_router = w_router.at[:, :N_GROUPS].set(router_group_w[i])
        w_router = w_router.at[:, N_GROUPS:N_GROUPS + N_EXPERTS].set(router_expert_w[i])
        xq, logits = _rmsnorm_router(h, norm_ffn[i], w_router)
        expert_idx, gates = _route(logits, router_group_b[i], router_expert_b[i])
        slot_tok, pos, plan = _dispatch(expert_idx)
        yb = _moe_experts(_moe_dispatch(xq, slot_tok, plan[3]), plan, i, expert_w_gate, expert_w_up, expert_w_down)
        h = _moe_combine(h, gates, yb, pos)

        p_i = jnp.concatenate([p_prompt[i].reshape(n_p, -1), p_sample[i].reshape(n_s, -1)], axis=0)
        xn = _rmsnorm(h, norm_ple[i], BF16)
        h = _matmul_ple(xn, ple_w_gate[i].astype(BF16), h, p_i.astype(BF16), ple_w_up[i].astype(BF16))

    y_p = _rmsnorm(h, norm_final, F32, 0, n_p)
    y_s = _rmsnorm(h, norm_final, F32, n_p, n_s)
    return (y_p.reshape(bp, tp, d), y_s.reshape(bs, ts, d), jnp.stack(states_p), jnp.stack(states_s),
            jnp.stack(k_p), jnp.stack(v_p), jnp.stack(k_s), jnp.stack(v_s))
```

```python
import functools
import math

import jax
import jax.numpy as jnp
import numpy as np
from jax import lax
from jax.experimental import pallas as pl
from jax.experimental.pallas import tpu as pltpu

F32 = jnp.float32
BF16 = jnp.bfloat16

EPS = 1e-6
HEAD = 128
STREAM_CHUNK = 64
N_GROUPS = 4
EXPERTS_PER_GROUP = 8
N_EXPERTS = N_GROUPS * EXPERTS_PER_GROUP
TOP_K = 2
ROUTER_LANES = 128
MOE_ROWS = 256
RANK_GROUP = 512
HGRN2_HEADS_PER_STEP = 8
DMA_ISSUE_UNROLL = 8
ATTN_Q_ROWS = 512
V7X_VMEM_LIMIT = 56 * 1024 * 1024
LOG2E = math.log2(math.e)


def _tile(dim, want):
    t = min(dim, want)
    while dim % t:
        t //= 2
    return t


def _params(sem):
    return pltpu.CompilerParams(dimension_semantics=sem, vmem_limit_bytes=V7X_VMEM_LIMIT)


def _split3(x):
    hi = x.astype(BF16)
    r1 = x - hi.astype(F32)
    mid = r1.astype(BF16)
    lo = (r1 - mid.astype(F32)).astype(BF16)
    return hi, mid, lo


def _dot(a, b):
    return jnp.dot(a, b, preferred_element_type=F32)


def _dot_nt(a, b):
    return lax.dot_general(a, b, (((1,), (1,)), ((), ())), preferred_element_type=F32)


def _rmsnorm_kernel(x_ref, g_ref, o_ref):
    x = x_ref[...]
    y = x * lax.rsqrt(jnp.mean(x * x, axis=-1, keepdims=True) + EPS)
    o_ref[...] = (y * g_ref[...]).astype(o_ref.dtype)


def _rmsnorm(x, g, out_dtype, row0=0, rows=None):
    n, d = x.shape
    rows = n if rows is None else rows
    tm = _tile(math.gcd(rows, row0) if row0 else rows, 512)
    blk0 = row0 // tm
    return pl.pallas_call(
        _rmsnorm_kernel,
        out_shape=jax.ShapeDtypeStruct((rows, d), out_dtype),
        grid=(rows // tm,),
        in_specs=[pl.BlockSpec((tm, d), lambda i: (blk0 + i, 0)), pl.BlockSpec((1, d), lambda i: (0, 0))],
        out_specs=pl.BlockSpec((tm, d), lambda i: (i, 0)),
        compiler_params=_params(("parallel",)),
        name="rmsnorm",
    )(x, g.reshape(1, d))


def _rmsnorm2_kernel(xa_ref, xb_ref, g_ref, o_ref, *, a_tiles):
    i = pl.program_id(0)

    @pl.when(i < a_tiles)
    def _():
        _rmsnorm_kernel(xa_ref, g_ref, o_ref)

    @pl.when(i >= a_tiles)
    def _():
        _rmsnorm_kernel(xb_ref, g_ref, o_ref)


def _rmsnorm2(xa, xb, g, out_dtype):
    na, d = xa.shape
    nb = xb.shape[0]
    tm = _tile(math.gcd(na, nb), 512)
    a_tiles = na // tm
    return pl.pallas_call(
        functools.partial(_rmsnorm2_kernel, a_tiles=a_tiles),
        out_shape=jax.ShapeDtypeStruct((na + nb, d), out_dtype),
        grid=((na + nb) // tm,),
        in_specs=[pl.BlockSpec((tm, d), lambda i: (jnp.minimum(i, a_tiles - 1), 0)),
                  pl.BlockSpec((tm, d), lambda i: (jnp.maximum(i - a_tiles, 0), 0)),
                  pl.BlockSpec((1, d), lambda i: (0, 0))],
        out_specs=pl.BlockSpec((tm, d), lambda i: (i, 0)),
        compiler_params=_params(("parallel",)),
        name="rmsnorm",
    )(xa, xb, g.reshape(1, d))


def _rmsnorm_router_kernel(x_ref, g_ref, w0_ref, w1_ref, w2_ref, o_ref, l_ref):
    x = x_ref[...]
    y = x * lax.rsqrt(jnp.mean(x * x, axis=-1, keepdims=True) + EPS)
    xn = y * g_ref[...]
    o_ref[...] = _pack_halves(xn)
    x0, x1, x2 = _split3(xn)
    w0, w1, w2 = w0_ref[...], w1_ref[...], w2_ref[...]
    small = _dot(x0, w2) + _dot(x1, w1) + _dot(x2, w0)
    mid = _dot(x0, w1) + _dot(x1, w0)
    l_ref[...] = _dot(x0, w0) + (mid + small)


def _rmsnorm_router(x, g, w_router):
    n, d = x.shape
    tm = _tile(n, 512)
    w0, w1, w2 = _split3(w_router)
    wspec = pl.BlockSpec((d, ROUTER_LANES), lambda i: (0, 0))
    return pl.pallas_call(
        _rmsnorm_router_kernel,
        out_shape=(jax.ShapeDtypeStruct((n, d // 2), jnp.uint32), jax.ShapeDtypeStruct((n, ROUTER_LANES), F32)),
        grid=(n // tm,),
        in_specs=[pl.BlockSpec((tm, d), lambda i: (i, 0)), pl.BlockSpec((1, d), lambda i: (0, 0)),
                  wspec, wspec, wspec],
        out_specs=(pl.BlockSpec((tm, d // 2), lambda i: (i, 0)), pl.BlockSpec((tm, ROUTER_LANES), lambda i: (i, 0))),
        compiler_params=_params(("parallel",)),
        name="rmsnorm_router",
    )(x, g.reshape(1, d), w0, w1, w2)


def _mm_kernel(x_ref, w_ref, o_ref, *, scale):
    acc = _dot(x_ref[...], w_ref[...])
    if scale is not None:
        acc = acc * scale
    o_ref[...] = acc.astype(o_ref.dtype)


def _mm_rowsplit_kernel(x_ref, w_ref, o_ref, o16_ref):
    acc = _dot(x_ref[...], w_ref[...])
    o16_ref[...] = acc.astype(o16_ref.dtype)
    for c in range(o_ref.shape[1]):
        o_ref[:, c, :] = acc[:, c * HEAD:(c + 1) * HEAD]


def _matmul_rowsplit(x, w, *, row0=0, rows=None, tm=1024, tn=1024):
    m, k = x.shape
    rows = m if rows is None else rows
    n = w.shape[1]
    tm = _tile(math.gcd(rows, row0) if row0 else rows, tm)
    tn = _tile(n, tn)
    blk0 = row0 // tm
    return pl.pallas_call(
        _mm_rowsplit_kernel,
        out_shape=(jax.ShapeDtypeStruct((rows, n // HEAD, HEAD), F32), jax.ShapeDtypeStruct((rows, n), BF16)),
        grid=(rows // tm, n // tn),
        in_specs=[pl.BlockSpec((tm, k), lambda i, j: (blk0 + i, 0)), pl.BlockSpec((k, tn), lambda i, j: (0, j))],
        out_specs=(pl.BlockSpec((tm, tn // HEAD, HEAD), lambda i, j: (i, j, 0)),
                   pl.BlockSpec((tm, tn), lambda i, j: (i, j))),
        compiler_params=_params(("parallel", "arbitrary")),
        name="matmul_rowsplit",
    )(x, w)


def _mm_res2_kernel(xa_ref, xb_ref, w_ref, ra_ref, rb_ref, o_ref, *, a_tiles):
    i = pl.program_id(0)

    @pl.when(i < a_tiles)
    def _():
        o_ref[...] = ra_ref[...] + _dot(xa_ref[...], w_ref[...])

    @pl.when(i >= a_tiles)
    def _():
        o_ref[...] = rb_ref[...] + _dot(xb_ref[...], w_ref[...])


def _mm_ple_kernel(x_ref, w_ref, r_ref, p_ref, wup_ref, o_ref):
    gate = jax.nn.sigmoid(_dot(x_ref[...], w_ref[...]))
    o_ref[...] = r_ref[...] + _dot(p_ref[...], wup_ref[...]) * gate


def _matmul(x, w, *, out_dtype=F32, scale=None, row0=0, rows=None, tm=1024, tn=1024):
    m, k = x.shape
    rows = m if rows is None else rows
    n = w.shape[1]
    tm = _tile(math.gcd(rows, row0) if row0 else rows, tm)
    tn = _tile(n, tn)
    blk0 = row0 // tm
    return pl.pallas_call(
        functools.partial(_mm_kernel, scale=scale),
        out_shape=jax.ShapeDtypeStruct((rows, n), out_dtype),
        grid=(rows // tm, n // tn),
        in_specs=[pl.BlockSpec((tm, k), lambda i, j: (blk0 + i, 0)), pl.BlockSpec((k, tn), lambda i, j: (0, j))],
        out_specs=pl.BlockSpec((tm, tn), lambda i, j: (i, j)),
        compiler_params=_params(("parallel", "arbitrary")),
        name="matmul",
    )(x, w)


def _matmul_residual2(xa, xb, w, res):
    ma, k = xa.shape
    mb = xb.shape[0]
    n = w.shape[1]
    tm, tn = _tile(math.gcd(ma, mb), 1024), _tile(n, 512)
    a_tiles, b_tiles = ma // tm, mb // tm
    in_a = lambda i: jnp.minimum(i, a_tiles - 1)
    in_b = lambda i: jnp.maximum(i - a_tiles, 0)
    if isinstance(res, tuple):
        b_off = 0
    else:
        res, b_off = (res, res), a_tiles
    return pl.pallas_call(
        functools.partial(_mm_res2_kernel, a_tiles=a_tiles),
        out_shape=jax.ShapeDtypeStruct((ma + mb, n), F32),
        grid=(a_tiles + b_tiles, n // tn),
        in_specs=[pl.BlockSpec((tm, k), lambda i, j: (in_a(i), 0)),
                  pl.BlockSpec((tm, k), lambda i, j: (in_b(i), 0)),
                  pl.BlockSpec((k, tn), lambda i, j: (0, j)),
                  pl.BlockSpec((tm, tn), lambda i, j: (in_a(i), jnp.where(i < a_tiles, j, n // tn - 1))),
                  pl.BlockSpec((tm, tn), lambda i, j: (b_off + in_b(i), jnp.where(i < a_tiles, 0, j)))],
        out_specs=pl.BlockSpec((tm, tn), lambda i, j: (i, j)),
        compiler_params=_params(("parallel", "arbitrary")),
        name="matmul_residual",
    )(xa, xb, w, *res)


def _matmul_ple(xn, w_gate, res, p, w_up, *, tm=1024, tn=512):
    m, k = xn.shape
    n = w_gate.shape[1]
    kp = p.shape[1]
    tm, tn = _tile(m, tm), _tile(n, tn)
    return pl.pallas_call(
        _mm_ple_kernel,
        out_shape=jax.ShapeDtypeStruct((m, n), F32),
        grid=(m // tm, n // tn),
        in_specs=[pl.BlockSpec((tm, k), lambda i, j: (i, 0)), pl.BlockSpec((k, tn), lambda i, j: (0, j)),
                  pl.BlockSpec((tm, tn), lambda i, j: (i, j)),
                  pl.BlockSpec((tm, kp), lambda i, j: (i, 0)), pl.BlockSpec((kp, tn), lambda i, j: (0, j))],
        out_specs=pl.BlockSpec((tm, tn), lambda i, j: (i, j)),
        compiler_params=_params(("parallel", "arbitrary")),
        name="matmul_ple",
    )(xn, w_gate, res, p, w_up)


def _hgrn2_level_table(c):
    t = np.arange(c)[:, None]
    s = np.arange(c)[None, :]
    x = np.bitwise_xor(t, s)
    lvl = np.where(x > 0, np.floor(np.log2(np.maximum(x, 1))).astype(np.int32), -1)
    return np.where(s < t, lvl, -1).astype(np.int32)


def _hgrn2_sign_table(c):
    t = np.arange(c)[None, :, None]
    j = np.arange(int(math.log2(c)))[:, None, None]
    return np.broadcast_to(np.where((t >> j) & 1, 1.0, -1.0), (j.shape[0], c, HEAD)).astype(np.float32)


def _hgrn2_kernel(q_ref, f_ref, v_ref, g_ref, lb_ref, gn_ref, lvl_ref, tri_ref, sgn_ref, *rest, chunk, n_chunks, hp,
                  has_s0):
    if has_s0:
        s0_ref, o_ref, s_ref, st_ref, b_ref = rest
    else:
        o_ref, s_ref, st_ref, b_ref = rest
    tb = pl.program_id(2)
    c = chunk

    @pl.when(tb == 0)
    def _():
        for hh in range(hp):
            st_ref[hh] = s0_ref[0, hh].T if has_s0 else jnp.zeros((HEAD, HEAD), F32)

    gn = gn_ref[...]
    lvl = lvl_ref[...]
    tri = tri_ref[...]
    sub = lax.broadcasted_iota(jnp.int32, (8, HEAD), 0)
    n_levels = int(math.log2(c))

    def gates(rows, hh):
        lanes = slice(hh * HEAD, (hh + 1) * HEAD)
        lb = lb_ref[:, lanes]
        q = jax.nn.silu(q_ref[rows, lanes])
        f = lb + (1.0 - lb) * jax.nn.sigmoid(f_ref[rows, lanes])
        k = 1.0 - f
        lg = jnp.log(f) * LOG2E
        l0, l1, l2 = _split3(lg)
        cs = _dot(tri, jnp.concatenate([l0, l1, l2], axis=1))
        b = cs[:, :HEAD] + (cs[:, HEAD:2 * HEAD] + cs[:, 2 * HEAD:])
        b_ref[hh] = b
        att = jnp.where(lvl == 0, _dot_nt((q * f).astype(BF16), k.astype(BF16)), 0.0)
        return q, k, b, att

    def level(j, hh, q, k, b, att):
        m = 1 << j
        pieces = []
        for g8 in range(c // 8):
            if m >= 4:
                r = (g8 * 8 // (2 * m)) * 2 * m + m - 1
                pieces.append(jnp.broadcast_to(b_ref[hh, r:r + 1, :], (8, HEAD)))
            else:
                top = jnp.broadcast_to(b_ref[hh, g8 * 8 + 1:g8 * 8 + 2, :], (8, HEAD))
                bot = jnp.broadcast_to(b_ref[hh, g8 * 8 + 5:g8 * 8 + 6, :], (8, HEAD))
                pieces.append(jnp.where(sub < 4, top, bot))
        e = jnp.exp2((b - jnp.concatenate(pieces, axis=0)) * sgn_ref[j])
        sj = _dot_nt((q * e).astype(BF16), (k * e).astype(BF16))
        return jnp.where(lvl == j, sj, att)

    def finish(rows, hh, q, k, b, att):
        lanes = slice(hh * HEAD, (hh + 1) * HEAD)
        v = v_ref[rows, lanes]
        st = st_ref[hh]
        b_end = b_ref[hh, c - 1:c, :]
        o = _dot(att.astype(BF16), v.astype(BF16))
        o = o + _dot_nt((q * jnp.exp2(b)).astype(BF16), st.astype(BF16))
        o = o + jnp.sum(q * k, axis=-1, keepdims=True) * v
        kd = (k * jnp.exp2(b_end - b)).astype(BF16)
        st_ref[hh] = jnp.exp2(b_end) * st + _dot(v.T.astype(BF16), kd)
        o = o * lax.rsqrt(jnp.mean(o * o, axis=-1, keepdims=True) + EPS) * gn
        o_ref[rows, lanes] = (o * jax.nn.silu(g_ref[rows, lanes])).astype(o_ref.dtype)

    def one_chunk(ci, carry):
        rows = pl.ds(pl.multiple_of(ci * c, c), c)
        work = [gates(rows, hh) for hh in range(hp)]
        for j in range(1, n_levels):
            work = [w[:3] + (level(j, hh, *w),) for hh, w in enumerate(work)]
        for hh, w in enumerate(work):
            finish(rows, hh, *w)
        return carry

    lax.fori_loop(0, n_chunks, one_chunk, 0)

    @pl.when(tb == pl.num_programs(2) - 1)
    def _():
        for hh in range(hp):
            s_ref[0, hh] = st_ref[hh].T


def _hgrn2(proj, row_off, bsz, t, lb, gnorm, s0):
    d = proj.shape[1] // 4
    nh = d // HEAD
    hp = _tile(nh, HGRN2_HEADS_PER_STEP)
    c = min(t, 128)
    n_levels = int(math.log2(c))
    tb = _tile(t, 512)
    n_tb = t // tb
    assert row_off % tb == 0 and tb % c == 0
    blk0 = row_off // tb
    ng = nh // hp
    col = lambda kind: pl.BlockSpec((tb, hp * HEAD), lambda b, h, i: (blk0 + b * n_tb + i, kind * ng + h))
    const = lambda shape: pl.BlockSpec(shape, lambda b, h, i: (0, 0))
    st_spec = pl.BlockSpec((1, hp, HEAD, HEAD), lambda b, h, i: (b, h, 0, 0))
    in_specs = [col(0), col(1), col(2), col(3), pl.BlockSpec((1, hp * HEAD), lambda b, h, i: (0, h)),
                const((1, HEAD)), const((c, c)), const((c, c)),
                pl.BlockSpec((n_levels, c, HEAD), lambda b, h, i: (0, 0, 0))]
    args = [proj, proj, proj, proj, lb.reshape(1, d), gnorm.reshape(1, HEAD),
            jnp.asarray(_hgrn2_level_table(c)), jnp.asarray(np.tril(np.ones((c, c), np.float32)), BF16),
            jnp.asarray(_hgrn2_sign_table(c))]
    if s0 is not None:
        in_specs.append(st_spec)
        args.append(s0)
    return pl.pallas_call(
        functools.partial(_hgrn2_kernel, chunk=c, n_chunks=tb // c, hp=hp, has_s0=s0 is not None),
        out_shape=(jax.ShapeDtypeStruct((bsz * t, d), BF16), jax.ShapeDtypeStruct((bsz, nh, HEAD, HEAD), F32)),
        grid=(bsz, ng, n_tb),
        in_specs=in_specs,
        out_specs=(pl.BlockSpec((tb, hp * HEAD), lambda b, h, i: (b * n_tb + i, h)), st_spec),
        scratch_shapes=[pltpu.VMEM((hp, HEAD, HEAD), F32), pltpu.VMEM((hp, c, HEAD), F32)],
        compiler_params=_params(("parallel", "parallel", "arbitrary")),
        name="hgrn2_scan",
    )(*args)


def _subln(o, g, post_scale):
    return o * lax.rsqrt(jnp.mean(o * o, axis=-1, keepdims=True) + EPS) * g * post_scale


def _attn_prompt_kernel(lam_ref, q_ref, kb_ref, v_ref, g_ref, o_ref, vb_ref, *, tq, n_q, post_scale):
    qi = pl.program_id(2)

    @pl.when(qi == 0)
    def _():
        vb_ref[...] = v_ref[...].astype(BF16)

    lam = lam_ref[0]
    q = q_ref[...]
    for vi in range(n_q):
        @pl.when(qi == vi)
        def _(head=vi * tq, ext=(vi + 1) * tq):
            pos = lax.broadcasted_iota(jnp.int32, (tq, tq), 0)
            allowed = lax.broadcasted_iota(jnp.int32, (tq, tq), 1) <= (pos | (STREAM_CHUNK - 1))
            v_tail = vb_ref[head:ext, :]

            def one_map(c0):
                qc = q[:, c0:c0 + HEAD]
                st = jnp.where(allowed, _dot_nt(qc, kb_ref[head:ext, c0:c0 + HEAD]), -jnp.inf)
                m = jnp.max(st, axis=-1, keepdims=True)
                if head:
                    sh = _dot_nt(qc, kb_ref[0:head, c0:c0 + HEAD])
                    m = jnp.maximum(m, jnp.max(sh, axis=-1, keepdims=True))
                et = jnp.exp2(st - m)
                tot = jnp.sum(et, axis=-1, keepdims=True)
                acc = _dot(et.astype(BF16), v_tail)
                if head:
                    eh = jnp.exp2(sh - m)
                    tot = tot + jnp.sum(eh, axis=-1, keepdims=True)
                    acc = acc + _dot(eh.astype(BF16), vb_ref[0:head, :])
                return acc / tot

            o = one_map(0) - lam * one_map(HEAD)
            o_ref[...] = _subln(o, g_ref[...], post_scale).astype(o_ref.dtype)


def _attn_prompt(q, k, v, lam, subln, post_scale, bsz, t):
    d = q.shape[1]
    hw = 2 * HEAD
    nh = d // hw
    tq = _tile(t, ATTN_Q_ROWS)
    n_q = t // tq
    return pl.pallas_call(
        functools.partial(_attn_prompt_kernel, tq=tq, n_q=n_q, post_scale=post_scale),
        out_shape=jax.ShapeDtypeStruct((bsz * t, d), BF16),
        grid=(bsz, nh, n_q),
        in_specs=[pl.BlockSpec(memory_space=pltpu.SMEM),
                  pl.BlockSpec((tq, hw), lambda b, h, i: (b * n_q + i, h)),
                  pl.BlockSpec((t, hw), lambda b, h, i: (b, h)),
                  pl.BlockSpec((t, hw), lambda b, h, i: (b, h)),
                  pl.BlockSpec((1, hw), lambda b, h, i: (0, 0))],
        out_specs=pl.BlockSpec((tq, hw), lambda b, h, i: (b * n_q + i, h)),
        scratch_shapes=[pltpu.VMEM((t, hw), BF16)],
        compiler_params=_params(("parallel", "parallel", "arbitrary")),
        name="diff_attn_prompt",
    )(lam.reshape(1), q, k, v, subln.reshape(1, hw))


def _attn_sample_kernel(lam_ref, q_ref, kn_ref, vn_ref, ck_ref, cv_ref, g_ref, o_ref, *, hg, post_scale):
    hw = 2 * HEAD
    lam = lam_ref[0]
    past = ck_ref.shape[0]
    ck_rows = ck_ref.reshape(past * 2 * hg, HEAD)
    for h in range(hg):
        lanes = slice(h * hw, (h + 1) * hw)
        v = cv_ref[:, lanes].astype(BF16)
        vn = vn_ref[:, lanes].astype(BF16)
        outs = []
        for c in range(2):
            col = slice((2 * h + c) * HEAD, (2 * h + c + 1) * HEAD)
            qc = q_ref[:, col]
            s = _dot_nt(qc, ck_rows[pl.ds(2 * h + c, past, stride=2 * hg), :].astype(BF16))
            sn = _dot_nt(qc, kn_ref[:, col])
            m = jnp.maximum(jnp.max(s, axis=-1, keepdims=True), jnp.max(sn, axis=-1, keepdims=True))
            e, en = jnp.exp2(s - m), jnp.exp2(sn - m)
            tot = jnp.sum(e, axis=-1, keepdims=True) + jnp.sum(en, axis=-1, keepdims=True)
            outs.append((_dot(e.astype(BF16), v) + _dot(en.astype(BF16), vn)) / tot)
        o = outs[0] - lam * outs[1]
        o_ref[:, lanes] = _subln(o, g_ref[...], post_scale).astype(o_ref.dtype)


def _attn_sample(q, q_row0, k, v, cache_k, cache_v, lam, subln, post_scale, bsz, t):
    d = q.shape[1]
    hw = 2 * HEAD
    nh = d // hw
    past = cache_k.shape[1]
    hg = _tile(nh, 4)
    assert (2 * hg) % 8 == 0 or hg == nh
    assert q_row0 % t == 0
    blk0 = q_row0 // t
    new = pl.BlockSpec((t, hg * hw), lambda b, g: (b, g))
    return pl.pallas_call(
        functools.partial(_attn_sample_kernel, hg=hg, post_scale=post_scale),
        out_shape=jax.ShapeDtypeStruct((bsz * t, d), BF16),
        grid=(bsz, nh // hg),
        in_specs=[pl.BlockSpec(memory_space=pltpu.SMEM),
                  pl.BlockSpec((t, hg * hw), lambda b, g: (blk0 + b, g)), new, new,
                  pl.BlockSpec((past, 2 * hg, HEAD), lambda b, g: (b, g, 0)),
                  pl.BlockSpec((past, hg * hw), lambda b, g: (b, g)),
                  pl.BlockSpec((1, hw), lambda b, g: (0, 0))],
        out_specs=pl.BlockSpec((t, hg * hw), lambda b, g: (b, g)),
        compiler_params=_params(("parallel", "parallel")),
        name="diff_attn_sample",
    )(lam.reshape(1), q, k, v, cache_k.reshape(bsz * past, 2 * nh, HEAD), cache_v.reshape(bsz * past, d),
      subln.reshape(1, hw))


def _expert_changed(be_ref):
    b = pl.program_id(1)
    return (b == 0) | (be_ref[b] != be_ref[jnp.maximum(b - 1, 0)])


def _pack_halves(x):
    half = x.shape[1] // 2
    as_bits = lambda v: lax.bitcast_convert_type(v.astype(BF16).astype(F32), jnp.uint32)
    return (as_bits(x[:, :half]) >> 16) | (as_bits(x[:, half:]) & jnp.uint32(0xFFFF0000))


def _unpack_halves(u):
    lo = lax.bitcast_convert_type(u << 16, F32).astype(BF16)
    hi = lax.bitcast_convert_type(u & jnp.uint32(0xFFFF0000), F32).astype(BF16)
    return lo, hi


def _moe_dispatch_kernel(tok_ref, nu_ref, x_hbm, o_ref, buf_ref, sem_ref, *, rows):
    i = pl.program_id(0)
    n_used = nu_ref[0]

    def copy(tile, r, slot):
        return pltpu.make_async_copy(x_hbm.at[tok_ref[tile * rows + r]], buf_ref.at[slot, r], sem_ref.at[slot])

    def start_tile(tile, slot):
        lax.fori_loop(0, rows, lambda r, c: (copy(tile, r, slot).start(), c)[1], 0, unroll=DMA_ISSUE_UNROLL)

    @pl.when(i == 0)
    def _():
        start_tile(0, 0)

    @pl.when(i + 1 < n_used)
    def _():
        start_tile(i + 1, (i + 1) % 2)

    @pl.when(i < n_used)
    def _():
        slot = i % 2
        lax.fori_loop(0, rows, lambda r, c: (copy(i, r, slot).wait(), c)[1], 0, unroll=DMA_ISSUE_UNROLL)
        o_ref[...] = buf_ref[slot]

    @pl.when(i >= n_used)
    def _():
        o_ref[...] = jnp.zeros(o_ref.shape, o_ref.dtype)


def _moe_dispatch(xq, slot_tok, n_used):
    w = xq.shape[1]
    p = slot_tok.shape[0]
    rows = MOE_ROWS
    return pl.pallas_call(
        functools.partial(_moe_dispatch_kernel, rows=rows),
        out_shape=jax.ShapeDtypeStruct((p, w), xq.dtype),
        grid_spec=pltpu.PrefetchScalarGridSpec(
            num_scalar_prefetch=2, grid=(p // rows,),
            in_specs=[pl.BlockSpec(memory_space=pl.ANY)],
            out_specs=pl.BlockSpec((rows, w), lambda i, tok, nu: (i, 0)),
            scratch_shapes=[pltpu.VMEM((2, rows, w), xq.dtype), pltpu.SemaphoreType.DMA((2,))]),
        compiler_params=_params(("arbitrary",)),
        name="moe_dispatch",
    )(slot_tok, n_used, xq)


def _moe_combine_kernel(pos_ref, h_ref, g_ref, gn_ref, yb_hbm, o_ref, on_ref, buf_ref, sem_ref, *, rows):
    i = pl.program_id(0)
    n_tiles = pl.num_programs(0)

    def copy(tile, r, k, slot):
        return pltpu.make_async_copy(yb_hbm.at[pos_ref[(tile * rows + r) * TOP_K + k]], buf_ref.at[slot, k, r],
                                     sem_ref.at[slot])

    def start_tile(tile, slot):
        def body(r, carry):
            for k in range(TOP_K):
                copy(tile, r, k, slot).start()
            return carry
        lax.fori_loop(0, rows, body, 0, unroll=DMA_ISSUE_UNROLL)

    @pl.when(i == 0)
    def _():
        start_tile(0, 0)

    @pl.when(i + 1 < n_tiles)
    def _():
        start_tile(i + 1, (i + 1) % 2)

    slot = i % 2

    def drain(r, carry):
        for k in range(TOP_K):
            copy(i, r, k, slot).wait()
        return carry

    lax.fori_loop(0, rows, drain, 0, unroll=DMA_ISSUE_UNROLL)
    g = g_ref[...]
    y = buf_ref[slot, 0] * g[:, 0:1]
    for k in range(1, TOP_K):
        y = y + buf_ref[slot, k] * g[:, k:k + 1]
    out = h_ref[...] + y
    o_ref[...] = out
    on = out * lax.rsqrt(jnp.mean(out * out, axis=-1, keepdims=True) + EPS)
    on_ref[...] = (on * gn_ref[...]).astype(on_ref.dtype)


def _moe_combine(h, gates, yb, pos, g_next):
    n, d = h.shape
    rows = _tile(n, 128)
    tile = pl.BlockSpec((rows, d), lambda i, pos: (i, 0))
    return pl.pallas_call(
        functools.partial(_moe_combine_kernel, rows=rows),
        out_shape=(jax.ShapeDtypeStruct((n, d), F32), jax.ShapeDtypeStruct((n, d), BF16)),
        grid_spec=pltpu.PrefetchScalarGridSpec(
            num_scalar_prefetch=1, grid=(n // rows,),
            in_specs=[tile, pl.BlockSpec((rows, TOP_K), lambda i, pos: (i, 0)),
                      pl.BlockSpec((1, d), lambda i, pos: (0, 0)), pl.BlockSpec(memory_space=pl.ANY)],
            out_specs=(tile, tile),
            scratch_shapes=[pltpu.VMEM((2, TOP_K, rows, d), F32), pltpu.SemaphoreType.DMA((2,))]),
        compiler_params=_params(("arbitrary",)),
        name="moe_combine",
    )(pos.reshape(-1), h, gates, g_next.reshape(1, d), yb)


def _moe_up_kernel(be_ref, we_ref, wj_ref, nu_ref, x_ref, wg_ref, wu_ref, o_ref, wgb_ref, wub_ref):
    del we_ref, wj_ref

    @pl.when(pl.program_id(1) < nu_ref[0])
    def _():
        @pl.when(_expert_changed(be_ref))
        def _():
            wgb_ref[...] = wg_ref[...].astype(BF16)
            wub_ref[...] = wu_ref[...].astype(BF16)

        xl, xh = _unpack_halves(x_ref[...])
        half = xl.shape[1]
        g = _dot(xl, wgb_ref[0:half, :]) + _dot(xh, wgb_ref[half:, :])
        u = _dot(xl, wub_ref[0:half, :]) + _dot(xh, wub_ref[half:, :])
        o_ref[...] = (jax.nn.silu(g) * u).astype(o_ref.dtype)

    @pl.when(pl.program_id(1) >= nu_ref[0])
    def _():
        o_ref[...] = jnp.zeros(o_ref.shape, o_ref.dtype)


def _moe_down_kernel(be_ref, we_ref, wj_ref, nu_ref, h_ref, wd_ref, o_ref, wdb_ref):
    del we_ref, wj_ref

    @pl.when(pl.program_id(1) < nu_ref[0])
    def _():
        @pl.when(_expert_changed(be_ref))
        def _():
            wdb_ref[...] = wd_ref[...].astype(BF16)

        o_ref[...] = _dot(h_ref[...], wdb_ref[...])

    @pl.when(pl.program_id(1) >= nu_ref[0])
    def _():
        o_ref[...] = jnp.zeros(o_ref.shape, o_ref.dtype)


def _moe_experts(xs, plan, layer, w_gate, w_up, w_down):
    blk_exp, w_exp, w_joff, n_used = plan
    p, w = xs.shape
    d = 2 * w
    de = w_gate.shape[3]
    nb = p // MOE_ROWS
    te = _tile(de, 512)
    row = lambda j, b, be, we, wj, nu: (jnp.minimum(b, nu[0] - 1), 0)
    out = lambda j, b, be, we, wj, nu: (b, j)

    def weight(n_j):
        return lambda j, b, be, we, wj, nu: (layer, we[b], 0, jnp.minimum(j + wj[b], n_j - 1))

    hidden = pl.pallas_call(
        _moe_up_kernel,
        out_shape=jax.ShapeDtypeStruct((p, de), BF16),
        grid_spec=pltpu.PrefetchScalarGridSpec(
            num_scalar_prefetch=4, grid=(de // te, nb),
            in_specs=[pl.BlockSpec((MOE_ROWS, w), row),
                      pl.BlockSpec((None, None, d, te), weight(de // te)),
                      pl.BlockSpec((None, None, d, te), weight(de // te))],
            out_specs=pl.BlockSpec((MOE_ROWS, te), out),
            scratch_shapes=[pltpu.VMEM((d, te), BF16), pltpu.VMEM((d, te), BF16)]),
        compiler_params=_params(("arbitrary", "arbitrary")),
        name="moe_gate_up",
    )(blk_exp, w_exp, w_joff, n_used, xs, w_gate, w_up)
    tn = _tile(d, 2048)
    return pl.pallas_call(
        _moe_down_kernel,
        out_shape=jax.ShapeDtypeStruct((p, d), F32),
        grid_spec=pltpu.PrefetchScalarGridSpec(
            num_scalar_prefetch=4, grid=(d // tn, nb),
            in_specs=[pl.BlockSpec((MOE_ROWS, de), row),
                      pl.BlockSpec((None, None, de, tn), weight(d // tn))],
            out_specs=pl.BlockSpec((MOE_ROWS, tn), out),
            scratch_shapes=[pltpu.VMEM((de, tn), BF16)]),
        compiler_params=_params(("arbitrary", "arbitrary")),
        name="moe_down",
    )(blk_exp, w_exp, w_joff, n_used, hidden, w_down)


def _slot_rank_kernel(e_ref, tri_ref, rank_ref, count_ref, carry_ref):
    i = pl.program_id(0)

    @pl.when(i == 0)
    def _():
        carry_ref[...] = jnp.zeros(carry_ref.shape, F32)

    onehot = e_ref[...] == lax.broadcasted_iota(jnp.int32, (e_ref.shape[0], ROUTER_LANES), 1)
    seen = _dot(tri_ref[...], jnp.where(onehot, 1.0, 0.0).astype(BF16)) + carry_ref[...]
    rank_ref[...] = (jnp.sum(jnp.where(onehot, seen, 0.0), axis=-1, keepdims=True) - 1.0).astype(jnp.int32)
    carry_ref[...] = seen[-1:, :]

    @pl.when(i == pl.num_programs(0) - 1)
    def _():
        count_ref[...] = seen[-1:, :].astype(jnp.int32)


def _slot_rank(flat_e):
    a = flat_e.shape[0]
    g = _tile(a, RANK_GROUP)
    rank, count = pl.pallas_call(
        _slot_rank_kernel,
        out_shape=(jax.ShapeDtypeStruct((a, 1), jnp.int32), jax.ShapeDtypeStruct((1, ROUTER_LANES), jnp.int32)),
        grid=(a // g,),
        in_specs=[pl.BlockSpec((g, 1), lambda i: (i, 0)), pl.BlockSpec((g, g), lambda i: (0, 0))],
        out_specs=(pl.BlockSpec((g, 1), lambda i: (i, 0)), pl.BlockSpec((1, ROUTER_LANES), lambda i: (0, 0))),
        scratch_shapes=[pltpu.VMEM((1, ROUTER_LANES), F32)],
        compiler_params=_params(("arbitrary",)),
        name="moe_slot_rank",
    )(flat_e.reshape(a, 1), jnp.asarray(np.tril(np.ones((g, g), np.float32)), BF16))
    return rank[:, 0], count[0, :N_EXPERTS]


def _top1(x):
    n = x.shape[-1]
    m = jnp.max(x, axis=-1, keepdims=True)
    idx = jnp.min(jnp.where(x == m, jnp.arange(n, dtype=jnp.int32), n), axis=-1, keepdims=True)
    return m, idx


def _route(logits, bg, be):
    n = logits.shape[0]
    pg = jax.nn.softmax(logits[:, :N_GROUPS] + bg, axis=-1)
    pg_top, g_top = _top1(pg)
    el = (logits[:, N_GROUPS:N_GROUPS + N_EXPERTS] + be).reshape(n, N_GROUPS, EXPERTS_PER_GROUP)
    sel = g_top[:, :, None] == jnp.arange(N_GROUPS, dtype=jnp.int32)[None, :, None]
    el_sel = jnp.sum(jnp.where(sel, el, 0.0), axis=1)
    pe = jax.nn.softmax(el_sel, axis=-1)
    p1, e1 = _top1(pe)
    lane = jnp.arange(EXPERTS_PER_GROUP, dtype=jnp.int32)[None, :]
    p2, e2 = _top1(jnp.where(lane == e1, -jnp.inf, pe))
    pe_top = jnp.concatenate([p1, p2], axis=-1)
    e_top = jnp.concatenate([e1, e2], axis=-1)
    gates = pg_top * pe_top / jnp.sum(pe_top, axis=-1, keepdims=True)
    return g_top * EXPERTS_PER_GROUP + e_top, gates


def _dispatch(expert_idx):
    n = expert_idx.shape[0]
    a = n * TOP_K
    flat_e = expert_idx.reshape(-1).astype(jnp.int32)
    rank, counts = _slot_rank(flat_e)
    padded = (counts + MOE_ROWS - 1) // MOE_ROWS * MOE_ROWS
    pend = jnp.cumsum(padded)
    pstart = pend - padded
    n_blocks = -(-(a + N_EXPERTS * (MOE_ROWS - 1)) // MOE_ROWS)
    pos = pstart.at[flat_e].get(mode="promise_in_bounds") + rank
    slot_tok = jnp.zeros((n_blocks * MOE_ROWS,), jnp.int32).at[pos].set(
        jnp.arange(a, dtype=jnp.int32) // TOP_K, mode="promise_in_bounds", unique_indices=True)

    n_used = pend[-1] // MOE_ROWS
    blk = jnp.minimum(jnp.arange(n_blocks, dtype=jnp.int32), n_used - 1)
    expert_of = lambda b: jnp.minimum(jnp.searchsorted(pend, b * MOE_ROWS, side='right'), N_EXPERTS - 1).astype(jnp.int32)
    blk_exp = expert_of(blk)
    first = (blk == 0) | (blk_exp != expert_of(jnp.maximum(blk - 1, 0)))
    nxt = pend[blk_exp] // MOE_ROWS
    wraps = nxt >= n_used
    nxt_exp = jnp.where(wraps, expert_of(jnp.zeros_like(nxt)), expert_of(jnp.minimum(nxt, n_used - 1)))
    w_exp = jnp.where(first, blk_exp, nxt_exp)
    w_joff = jnp.where(first, 0, wraps.astype(jnp.int32))
    plan = (blk_exp, w_exp.astype(jnp.int32), w_joff.astype(jnp.int32), n_used.reshape(1).astype(jnp.int32))
    return slot_tok, pos.reshape(n, TOP_K), plan


def kernel(x_prompt, x_sample, state_hgrn, cache_k, cache_v, p_prompt, p_sample, norm_mix, norm_ffn, norm_ple, norm_final, a_w_in, a_w_out, a_lb_logits, a_gnorm, b_w_in, b_w_out, b_lambda_q1, b_lambda_k1, b_lambda_q2, b_lambda_k2, b_subln, router_group_w, router_group_b, router_expert_w, router_expert_b, expert_w_gate, expert_w_up, expert_w_down, ple_w_up, ple_w_gate):
    bp, tp, d = x_prompt.shape
    bs, ts, _ = x_sample.shape
    n_p, n_s = bp * tp, bs * ts
    depth = norm_mix.shape[0]
    nh_b = d // (2 * HEAD)

    h = (x_prompt.reshape(n_p, d), x_sample.reshape(n_s, d))
    lb_all = jnp.cumsum(jax.nn.softmax(a_lb_logits.astype(F32), axis=0), axis=0)
    states_p, states_s, k_p, v_p, k_s, v_s = [], [], [], [], [], []

    for i in range(depth):
        j = i // 2
        xn = _rmsnorm2(*h, norm_mix[i], BF16) if isinstance(h, tuple) else _rmsnorm(h, norm_mix[i], BF16)
        if i % 2 == 0:
            proj = _matmul(xn, a_w_in[j].astype(BF16))
            o_p, s_p = _hgrn2(proj, 0, bp, tp, lb_all[j], a_gnorm[j], None)
            o_s, s_s = _hgrn2(proj, n_p, bs, ts, lb_all[j], a_gnorm[j], state_hgrn[j])
            states_p.append(s_p)
            states_s.append(s_s)
            w_out = a_w_out[j]
        else:
            w_in = b_w_in[j]
            wk, wv = w_in[:, d:2 * d].astype(BF16), w_in[:, 2 * d:].astype(BF16)
            q = _matmul(xn, w_in[:, :d].astype(BF16), out_dtype=BF16, scale=HEAD ** -0.5 * LOG2E)
            (kp, kp16), vp = _matmul_rowsplit(xn, wk, rows=n_p), _matmul(xn, wv, rows=n_p)
            (ks, ks16), vs = _matmul_rowsplit(xn, wk, row0=n_p, rows=n_s), _matmul(xn, wv, row0=n_p, rows=n_s)
            lam0 = 0.8 - 0.6 * math.exp(-0.3 * i)
            lam = (jnp.exp(jnp.sum(b_lambda_q1[j] * b_lambda_k1[j]))
                   - jnp.exp(jnp.sum(b_lambda_q2[j] * b_lambda_k2[j])) + lam0).astype(F32)
            o_p = _attn_prompt(q, kp16, vp, lam, b_subln[j], 1.0 - lam0, bp, tp)
            o_s = _attn_sample(q, n_p, ks16, vs, cache_k[j], cache_v[j], lam, b_subln[j], 1.0 - lam0, bs, ts)
            k_p.append(kp.reshape(bp, tp, nh_b, 2, HEAD))
            v_p.append(vp.reshape(bp, tp, nh_b, 2 * HEAD))
            k_s.append(ks.reshape(bs, ts, nh_b, 2, HEAD))
            v_s.append(vs.reshape(bs, ts, nh_b, 2 * HEAD))
            w_out = b_w_out[j]
        h = _matmul_residual2(o_p, o_s, w_out.astype(BF16), h)

        w_router = jnp.zeros((d, ROUTER_LANES), F32)
        w_router = w_router.at[:, :N_GROUPS].set(router_group_w[i])
        w_router = w_router.at[:, N_GROUPS:N_GROUPS + N_EXPERTS].set(router_expert_w[i])
        xq, logits = _rmsnorm_router(h, norm_ffn[i], w_router)
        expert_idx, gates = _route(logits, router_group_b[i], router_expert_b[i])
        slot_tok, pos, plan = _dispatch(expert_idx)
        yb = _moe_experts(_moe_dispatch(xq, slot_tok, plan[3]), plan, i, expert_w_gate, expert_w_up, expert_w_down)
        h, xn = _moe_combine(h, gates, yb, pos, norm_ple[i])

        p_i = jnp.concatenate([p_prompt[i].reshape(n_p, -1), p_sample[i].reshape(n_s, -1)], axis=0)
        h = _matmul_ple(xn, ple_w_gate[i].astype(BF16), h, p_i.astype(BF16), ple_w_up[i].astype(BF16))

    y_p = _rmsnorm(h, norm_final, F32, 0, n_p)
    y_s = _rmsnorm(h, norm_final, F32, n_p, n_s)
    return (y_p.reshape(bp, tp, d), y_s.reshape(bs, ts, d), jnp.stack(states_p), jnp.stack(states_s),
            jnp.stack(k_p), jnp.stack(v_p), jnp.stack(k_s), jnp.stack(v_s))
```

```python
import functools
import math

import jax
import jax.numpy as jnp
import numpy as np
from jax import lax
from jax.experimental import pallas as pl
from jax.experimental.pallas import tpu as pltpu

F32 = jnp.float32
BF16 = jnp.bfloat16

EPS = 1e-6
HEAD = 128
STREAM_CHUNK = 64
N_GROUPS = 4
EXPERTS_PER_GROUP = 8
N_EXPERTS = N_GROUPS * EXPERTS_PER_GROUP
TOP_K = 2
ROUTER_LANES = 128
MOE_ROWS = 256
RANK_GROUP = 512
HGRN2_HEADS_PER_STEP = 8
DMA_ISSUE_UNROLL = 8
ATTN_Q_ROWS = 512
V7X_VMEM_LIMIT = 56 * 1024 * 1024
LOG2E = math.log2(math.e)


def _tile(dim, want):
    t = min(dim, want)
    while dim % t:
        t //= 2
    return t


def _params(sem):
    return pltpu.CompilerParams(dimension_semantics=sem, vmem_limit_bytes=V7X_VMEM_LIMIT)


def _split3(x):
    hi = x.astype(BF16)
    r1 = x - hi.astype(F32)
    mid = r1.astype(BF16)
    lo = (r1 - mid.astype(F32)).astype(BF16)
    return hi, mid, lo


def _dot(a, b):
    return jnp.dot(a, b, preferred_element_type=F32)


def _dot_nt(a, b):
    return lax.dot_general(a, b, (((1,), (1,)), ((), ())), preferred_element_type=F32)


def _rmsnorm_kernel(x_ref, g_ref, o_ref):
    x = x_ref[...]
    y = x * lax.rsqrt(jnp.mean(x * x, axis=-1, keepdims=True) + EPS)
    o_ref[...] = (y * g_ref[...]).astype(o_ref.dtype)


def _rmsnorm(x, g, out_dtype, row0=0, rows=None):
    n, d = x.shape
    rows = n if rows is None else rows
    tm = _tile(math.gcd(rows, row0) if row0 else rows, 512)
    blk0 = row0 // tm
    return pl.pallas_call(
        _rmsnorm_kernel,
        out_shape=jax.ShapeDtypeStruct((rows, d), out_dtype),
        grid=(rows // tm,),
        in_specs=[pl.BlockSpec((tm, d), lambda i: (blk0 + i, 0)), pl.BlockSpec((1, d), lambda i: (0, 0))],
        out_specs=pl.BlockSpec((tm, d), lambda i: (i, 0)),
        compiler_params=_params(("parallel",)),
        name="rmsnorm",
    )(x, g.reshape(1, d))


def _rmsnorm2_kernel(xa_ref, xb_ref, g_ref, o_ref, *, a_tiles):
    i = pl.program_id(0)

    @pl.when(i < a_tiles)
    def _():
        _rmsnorm_kernel(xa_ref, g_ref, o_ref)

    @pl.when(i >= a_tiles)
    def _():
        _rmsnorm_kernel(xb_ref, g_ref, o_ref)


def _rmsnorm2(xa, xb, g, out_dtype):
    na, d = xa.shape
    nb = xb.shape[0]
    tm = _tile(math.gcd(na, nb), 512)
    a_tiles = na // tm
    return pl.pallas_call(
        functools.partial(_rmsnorm2_kernel, a_tiles=a_tiles),
        out_shape=jax.ShapeDtypeStruct((na + nb, d), out_dtype),
        grid=((na + nb) // tm,),
        in_specs=[pl.BlockSpec((tm, d), lambda i: (jnp.minimum(i, a_tiles - 1), 0)),
                  pl.BlockSpec((tm, d), lambda i: (jnp.maximum(i - a_tiles, 0), 0)),
                  pl.BlockSpec((1, d), lambda i: (0, 0))],
        out_specs=pl.BlockSpec((tm, d), lambda i: (i, 0)),
        compiler_params=_params(("parallel",)),
        name="rmsnorm",
    )(xa, xb, g.reshape(1, d))


def _first_lane(cond, lane):
    return jnp.min(jnp.where(cond, lane, float(ROUTER_LANES)), axis=-1, keepdims=True)


def _route_lanes(logits):
    lane = lax.broadcasted_iota(jnp.int32, logits.shape, 1).astype(F32)
    is_group = lane < N_GROUPS

    def softmax_over(mask):
        z = jnp.where(mask, logits, -jnp.inf)
        e = jnp.exp(z - jnp.max(z, axis=-1, keepdims=True))
        return e / jnp.sum(e, axis=-1, keepdims=True)

    pg = softmax_over(is_group)
    pg_top = jnp.max(pg, axis=-1, keepdims=True)
    g_top = _first_lane((pg == pg_top) & is_group, lane)
    lo = N_GROUPS + EXPERTS_PER_GROUP * g_top
    in_group = (lane >= lo) & (lane < lo + EXPERTS_PER_GROUP)
    pe = jnp.where(in_group, softmax_over(in_group), -1.0)
    p1 = jnp.max(pe, axis=-1, keepdims=True)
    e1 = _first_lane(pe == p1, lane)
    pe_rest = jnp.where(lane == e1, -1.0, pe)
    p2 = jnp.max(pe_rest, axis=-1, keepdims=True)
    e2 = _first_lane(pe_rest == p2, lane)
    scale = pg_top / (p1 + p2)
    out = jnp.where(lane == 0, p1 * scale, jnp.where(lane == 1, p2 * scale, 0.0))
    return jnp.where(lane == 2, e1 - N_GROUPS, jnp.where(lane == 3, e2 - N_GROUPS, out))


def _rmsnorm_router_kernel(x_ref, g_ref, w0_ref, w1_ref, w2_ref, b_ref, o_ref, r_ref):
    x = x_ref[...]
    y = x * lax.rsqrt(jnp.mean(x * x, axis=-1, keepdims=True) + EPS)
    xn = y * g_ref[...]
    o_ref[...] = _pack_halves(xn)
    x0, x1, x2 = _split3(xn)
    w0, w1, w2 = w0_ref[...], w1_ref[...], w2_ref[...]
    small = _dot(x0, w2) + _dot(x1, w1) + _dot(x2, w0)
    mid = _dot(x0, w1) + _dot(x1, w0)
    r_ref[...] = _route_lanes(_dot(x0, w0) + (mid + small) + b_ref[...])


def _rmsnorm_router(x, g, w_router, b_router):
    n, d = x.shape
    tm = _tile(n, 512)
    w0, w1, w2 = _split3(w_router)
    wspec = pl.BlockSpec((d, ROUTER_LANES), lambda i: (0, 0))
    xq, route = pl.pallas_call(
        _rmsnorm_router_kernel,
        out_shape=(jax.ShapeDtypeStruct((n, d // 2), jnp.uint32), jax.ShapeDtypeStruct((n, ROUTER_LANES), F32)),
        grid=(n // tm,),
        in_specs=[pl.BlockSpec((tm, d), lambda i: (i, 0)), pl.BlockSpec((1, d), lambda i: (0, 0)),
                  wspec, wspec, wspec, pl.BlockSpec((1, ROUTER_LANES), lambda i: (0, 0))],
        out_specs=(pl.BlockSpec((tm, d // 2), lambda i: (i, 0)), pl.BlockSpec((tm, ROUTER_LANES), lambda i: (i, 0))),
        compiler_params=_params(("parallel",)),
        name="rmsnorm_router",
    )(x, g.reshape(1, d), w0, w1, w2, b_router.reshape(1, ROUTER_LANES))
    return xq, route[:, TOP_K:2 * TOP_K].astype(jnp.int32), route[:, :TOP_K]


def _mm_kernel(x_ref, w_ref, o_ref, *, scale):
    acc = _dot(x_ref[...], w_ref[...])
    if scale is not None:
        acc = acc * scale
    o_ref[...] = acc.astype(o_ref.dtype)


def _mm_rowsplit_kernel(x_ref, w_ref, o_ref, o16_ref):
    acc = _dot(x_ref[...], w_ref[...])
    o16_ref[...] = acc.astype(o16_ref.dtype)
    for c in range(o_ref.shape[1]):
        o_ref[:, c, :] = acc[:, c * HEAD:(c + 1) * HEAD]


def _matmul_rowsplit(x, w, *, row0=0, rows=None, tm=1024, tn=1024):
    m, k = x.shape
    rows = m if rows is None else rows
    n = w.shape[1]
    tm = _tile(math.gcd(rows, row0) if row0 else rows, tm)
    tn = _tile(n, tn)
    blk0 = row0 // tm
    return pl.pallas_call(
        _mm_rowsplit_kernel,
        out_shape=(jax.ShapeDtypeStruct((rows, n // HEAD, HEAD), F32), jax.ShapeDtypeStruct((rows, n), BF16)),
        grid=(rows // tm, n // tn),
        in_specs=[pl.BlockSpec((tm, k), lambda i, j: (blk0 + i, 0)), pl.BlockSpec((k, tn), lambda i, j: (0, j))],
        out_specs=(pl.BlockSpec((tm, tn // HEAD, HEAD), lambda i, j: (i, j, 0)),
                   pl.BlockSpec((tm, tn), lambda i, j: (i, j))),
        compiler_params=_params(("parallel", "arbitrary")),
        name="matmul_rowsplit",
    )(x, w)


def _mm_res2_kernel(xa_ref, xb_ref, w_ref, ra_ref, rb_ref, o_ref, *, a_tiles):
    i = pl.program_id(0)

    @pl.when(i < a_tiles)
    def _():
        o_ref[...] = ra_ref[...] + _dot(xa_ref[...], w_ref[...])

    @pl.when(i >= a_tiles)
    def _():
        o_ref[...] = rb_ref[...] + _dot(xb_ref[...], w_ref[...])


def _mm_ple_kernel(x_ref, w_ref, r_ref, p_ref, wup_ref, o_ref):
    gate = jax.nn.sigmoid(_dot(x_ref[...], w_ref[...]))
    o_ref[...] = r_ref[...] + _dot(p_ref[...], wup_ref[...]) * gate


def _matmul(x, w, *, out_dtype=F32, scale=None, row0=0, rows=None, tm=1024, tn=1024):
    m, k = x.shape
    rows = m if rows is None else rows
    n = w.shape[1]
    tm = _tile(math.gcd(rows, row0) if row0 else rows, tm)
    tn = _tile(n, tn)
    blk0 = row0 // tm
    return pl.pallas_call(
        functools.partial(_mm_kernel, scale=scale),
        out_shape=jax.ShapeDtypeStruct((rows, n), out_dtype),
        grid=(rows // tm, n // tn),
        in_specs=[pl.BlockSpec((tm, k), lambda i, j: (blk0 + i, 0)), pl.BlockSpec((k, tn), lambda i, j: (0, j))],
        out_specs=pl.BlockSpec((tm, tn), lambda i, j: (i, j)),
        compiler_params=_params(("parallel", "arbitrary")),
        name="matmul",
    )(x, w)


def _matmul_residual2(xa, xb, w, res):
    ma, k = xa.shape
    mb = xb.shape[0]
    n = w.shape[1]
    tm, tn = _tile(math.gcd(ma, mb), 1024), _tile(n, 512)
    a_tiles, b_tiles = ma // tm, mb // tm
    in_a = lambda i: jnp.minimum(i, a_tiles - 1)
    in_b = lambda i: jnp.maximum(i - a_tiles, 0)
    if isinstance(res, tuple):
        b_off = 0
    else:
        res, b_off = (res, res), a_tiles
    return pl.pallas_call(
        functools.partial(_mm_res2_kernel, a_tiles=a_tiles),
        out_shape=jax.ShapeDtypeStruct((ma + mb, n), F32),
        grid=(a_tiles + b_tiles, n // tn),
        in_specs=[pl.BlockSpec((tm, k), lambda i, j: (in_a(i), 0)),
                  pl.BlockSpec((tm, k), lambda i, j: (in_b(i), 0)),
                  pl.BlockSpec((k, tn), lambda i, j: (0, j)),
                  pl.BlockSpec((tm, tn), lambda i, j: (in_a(i), jnp.where(i < a_tiles, j, n // tn - 1))),
                  pl.BlockSpec((tm, tn), lambda i, j: (b_off + in_b(i), jnp.where(i < a_tiles, 0, j)))],
        out_specs=pl.BlockSpec((tm, tn), lambda i, j: (i, j)),
        compiler_params=_params(("parallel", "arbitrary")),
        name="matmul_residual",
    )(xa, xb, w, *res)


def _matmul_ple(xn, w_gate, res, p, w_up, *, tm=1024, tn=512):
    m, k = xn.shape
    n = w_gate.shape[1]
    kp = p.shape[1]
    tm, tn = _tile(m, tm), _tile(n, tn)
    return pl.pallas_call(
        _mm_ple_kernel,
        out_shape=jax.ShapeDtypeStruct((m, n), F32),
        grid=(m // tm, n // tn),
        in_specs=[pl.BlockSpec((tm, k), lambda i, j: (i, 0)), pl.BlockSpec((k, tn), lambda i, j: (0, j)),
                  pl.BlockSpec((tm, tn), lambda i, j: (i, j)),
                  pl.BlockSpec((tm, kp), lambda i, j: (i, 0)), pl.BlockSpec((kp, tn), lambda i, j: (0, j))],
        out_specs=pl.BlockSpec((tm, tn), lambda i, j: (i, j)),
        compiler_params=_params(("parallel", "arbitrary")),
        name="matmul_ple",
    )(xn, w_gate, res, p, w_up)


def _hgrn2_level_table(c):
    t = np.arange(c)[:, None]
    s = np.arange(c)[None, :]
    x = np.bitwise_xor(t, s)
    lvl = np.where(x > 0, np.floor(np.log2(np.maximum(x, 1))).astype(np.int32), -1)
    return np.where(s < t, lvl, -1).astype(np.int32)


def _hgrn2_sign_table(c):
    t = np.arange(c)[None, :, None]
    j = np.arange(int(math.log2(c)))[:, None, None]
    return np.broadcast_to(np.where((t >> j) & 1, 1.0, -1.0), (j.shape[0], c, HEAD)).astype(np.float32)


def _hgrn2_kernel(q_ref, f_ref, v_ref, g_ref, lb_ref, gn_ref, lvl_ref, tri_ref, sgn_ref, *rest, chunk, n_chunks, hp,
                  has_s0):
    if has_s0:
        s0_ref, o_ref, s_ref, st_ref, b_ref = rest
    else:
        o_ref, s_ref, st_ref, b_ref = rest
    tb = pl.program_id(2)
    c = chunk

    @pl.when(tb == 0)
    def _():
        for hh in range(hp):
            st_ref[hh] = s0_ref[0, hh].T if has_s0 else jnp.zeros((HEAD, HEAD), F32)

    gn = gn_ref[...]
    lvl = lvl_ref[...]
    tri = tri_ref[...]
    sub = lax.broadcasted_iota(jnp.int32, (8, HEAD), 0)
    n_levels = int(math.log2(c))

    def gates(rows, hh):
        lanes = slice(hh * HEAD, (hh + 1) * HEAD)
        lb = lb_ref[:, lanes]
        q = jax.nn.silu(q_ref[rows, lanes])
        f = lb + (1.0 - lb) * jax.nn.sigmoid(f_ref[rows, lanes])
        k = 1.0 - f
        lg = jnp.log(f) * LOG2E
        l0, l1, l2 = _split3(lg)
        cs = _dot(tri, jnp.concatenate([l0, l1, l2], axis=1))
        b = cs[:, :HEAD] + (cs[:, HEAD:2 * HEAD] + cs[:, 2 * HEAD:])
        b_ref[hh] = b
        att = jnp.where(lvl == 0, _dot_nt((q * f).astype(BF16), k.astype(BF16)), 0.0)
        return q, k, b, att

    def level(j, hh, q, k, b, att):
        m = 1 << j
        pieces = []
        for g8 in range(c // 8):
            if m >= 4:
                r = (g8 * 8 // (2 * m)) * 2 * m + m - 1
                pieces.append(jnp.broadcast_to(b_ref[hh, r:r + 1, :], (8, HEAD)))
            else:
                top = jnp.broadcast_to(b_ref[hh, g8 * 8 + 1:g8 * 8 + 2, :], (8, HEAD))
                bot = jnp.broadcast_to(b_ref[hh, g8 * 8 + 5:g8 * 8 + 6, :], (8, HEAD))
                pieces.append(jnp.where(sub < 4, top, bot))
        e = jnp.exp2((b - jnp.concatenate(pieces, axis=0)) * sgn_ref[j])
        sj = _dot_nt((q * e).astype(BF16), (k * e).astype(BF16))
        return jnp.where(lvl == j, sj, att)

    def finish(rows, hh, q, k, b, att):
        lanes = slice(hh * HEAD, (hh + 1) * HEAD)
        v = v_ref[rows, lanes]
        st = st_ref[hh]
        b_end = b_ref[hh, c - 1:c, :]
        o = _dot(att.astype(BF16), v.astype(BF16))
        o = o + _dot_nt((q * jnp.exp2(b)).astype(BF16), st.astype(BF16))
        o = o + jnp.sum(q * k, axis=-1, keepdims=True) * v
        kd = (k * jnp.exp2(b_end - b)).astype(BF16)
        st_ref[hh] = jnp.exp2(b_end) * st + _dot(v.T.astype(BF16), kd)
        o = o * lax.rsqrt(jnp.mean(o * o, axis=-1, keepdims=True) + EPS) * gn
        o_ref[rows, lanes] = (o * jax.nn.silu(g_ref[rows, lanes])).astype(o_ref.dtype)

    def one_chunk(ci, carry):
        rows = pl.ds(pl.multiple_of(ci * c, c), c)
        work = [gates(rows, hh) for hh in range(hp)]
        for j in range(1, n_levels):
            work = [w[:3] + (level(j, hh, *w),) for hh, w in enumerate(work)]
        for hh, w in enumerate(work):
            finish(rows, hh, *w)
        return carry

    lax.fori_loop(0, n_chunks, one_chunk, 0)

    @pl.when(tb == pl.num_programs(2) - 1)
    def _():
        for hh in range(hp):
            s_ref[0, hh] = st_ref[hh].T


def _hgrn2(proj, row_off, bsz, t, lb, gnorm, s0):
    d = proj.shape[1] // 4
    nh = d // HEAD
    hp = _tile(nh, HGRN2_HEADS_PER_STEP)
    c = min(t, 128)
    n_levels = int(math.log2(c))
    tb = _tile(t, 512)
    n_tb = t // tb
    assert row_off % tb == 0 and tb % c == 0
    blk0 = row_off // tb
    ng = nh // hp
    col = lambda kind: pl.BlockSpec((tb, hp * HEAD), lambda b, h, i: (blk0 + b * n_tb + i, kind * ng + h))
    const = lambda shape: pl.BlockSpec(shape, lambda b, h, i: (0, 0))
    st_spec = pl.BlockSpec((1, hp, HEAD, HEAD), lambda b, h, i: (b, h, 0, 0))
    in_specs = [col(0), col(1), col(2), col(3), pl.BlockSpec((1, hp * HEAD), lambda b, h, i: (0, h)),
                const((1, HEAD)), const((c, c)), const((c, c)),
                pl.BlockSpec((n_levels, c, HEAD), lambda b, h, i: (0, 0, 0))]
    args = [proj, proj, proj, proj, lb.reshape(1, d), gnorm.reshape(1, HEAD),
            jnp.asarray(_hgrn2_level_table(c)), jnp.asarray(np.tril(np.ones((c, c), np.float32)), BF16),
            jnp.asarray(_hgrn2_sign_table(c))]
    if s0 is not None:
        in_specs.append(st_spec)
        args.append(s0)
    return pl.pallas_call(
        functools.partial(_hgrn2_kernel, chunk=c, n_chunks=tb // c, hp=hp, has_s0=s0 is not None),
        out_shape=(jax.ShapeDtypeStruct((bsz * t, d), BF16), jax.ShapeDtypeStruct((bsz, nh, HEAD, HEAD), F32)),
        grid=(bsz, ng, n_tb),
        in_specs=in_specs,
        out_specs=(pl.BlockSpec((tb, hp * HEAD), lambda b, h, i: (b * n_tb + i, h)), st_spec),
        scratch_shapes=[pltpu.VMEM((hp, HEAD, HEAD), F32), pltpu.VMEM((hp, c, HEAD), F32)],
        compiler_params=_params(("parallel", "parallel", "arbitrary")),
        name="hgrn2_scan",
    )(*args)


def _subln(o, g, post_scale):
    return o * lax.rsqrt(jnp.mean(o * o, axis=-1, keepdims=True) + EPS) * g * post_scale


def _attn_prompt_kernel(lam_ref, q_ref, kb_ref, v_ref, g_ref, o_ref, vb_ref, *, tq, n_q, post_scale):
    qi = pl.program_id(2)

    @pl.when(qi == 0)
    def _():
        vb_ref[...] = v_ref[...].astype(BF16)

    lam = lam_ref[0]
    q = q_ref[...]
    for vi in range(n_q):
        @pl.when(qi == vi)
        def _(head=vi * tq, ext=(vi + 1) * tq):
            pos = lax.broadcasted_iota(jnp.int32, (tq, tq), 0)
            allowed = lax.broadcasted_iota(jnp.int32, (tq, tq), 1) <= (pos | (STREAM_CHUNK - 1))
            v_tail = vb_ref[head:ext, :]

            def one_map(c0):
                qc = q[:, c0:c0 + HEAD]
                st = jnp.where(allowed, _dot_nt(qc, kb_ref[head:ext, c0:c0 + HEAD]), -jnp.inf)
                m = jnp.max(st, axis=-1, keepdims=True)
                if head:
                    sh = _dot_nt(qc, kb_ref[0:head, c0:c0 + HEAD])
                    m = jnp.maximum(m, jnp.max(sh, axis=-1, keepdims=True))
                et = jnp.exp2(st - m)
                tot = jnp.sum(et, axis=-1, keepdims=True)
                acc = _dot(et.astype(BF16), v_tail)
                if head:
                    eh = jnp.exp2(sh - m)
                    tot = tot + jnp.sum(eh, axis=-1, keepdims=True)
                    acc = acc + _dot(eh.astype(BF16), vb_ref[0:head, :])
                return acc / tot

            o = one_map(0) - lam * one_map(HEAD)
            o_ref[...] = _subln(o, g_ref[...], post_scale).astype(o_ref.dtype)


def _attn_prompt(q, k, v, lam, subln, post_scale, bsz, t):
    d = q.shape[1]
    hw = 2 * HEAD
    nh = d // hw
    tq = _tile(t, ATTN_Q_ROWS)
    n_q = t // tq
    return pl.pallas_call(
        functools.partial(_attn_prompt_kernel, tq=tq, n_q=n_q, post_scale=post_scale),
        out_shape=jax.ShapeDtypeStruct((bsz * t, d), BF16),
        grid=(bsz, nh, n_q),
        in_specs=[pl.BlockSpec(memory_space=pltpu.SMEM),
                  pl.BlockSpec((tq, hw), lambda b, h, i: (b * n_q + i, h)),
                  pl.BlockSpec((t, hw), lambda b, h, i: (b, h)),
                  pl.BlockSpec((t, hw), lambda b, h, i: (b, h)),
                  pl.BlockSpec((1, hw), lambda b, h, i: (0, 0))],
        out_specs=pl.BlockSpec((tq, hw), lambda b, h, i: (b * n_q + i, h)),
        scratch_shapes=[pltpu.VMEM((t, hw), BF16)],
        compiler_params=_params(("parallel", "parallel", "arbitrary")),
        name="diff_attn_prompt",
    )(lam.reshape(1), q, k, v, subln.reshape(1, hw))


def _attn_sample_kernel(lam_ref, q_ref, kn_ref, vn_ref, ck_ref, cv_ref, g_ref, o_ref, *, hg, post_scale):
    hw = 2 * HEAD
    lam = lam_ref[0]
    past = ck_ref.shape[0]
    ck_rows = ck_ref.reshape(past * 2 * hg, HEAD)
    for h in range(hg):
        lanes = slice(h * hw, (h + 1) * hw)
        v = cv_ref[:, lanes].astype(BF16)
        vn = vn_ref[:, lanes].astype(BF16)
        outs = []
        for c in range(2):
            col = slice((2 * h + c) * HEAD, (2 * h + c + 1) * HEAD)
            qc = q_ref[:, col]
            s = _dot_nt(qc, ck_rows[pl.ds(2 * h + c, past, stride=2 * hg), :].astype(BF16))
            sn = _dot_nt(qc, kn_ref[:, col])
            m = jnp.maximum(jnp.max(s, axis=-1, keepdims=True), jnp.max(sn, axis=-1, keepdims=True))
            e, en = jnp.exp2(s - m), jnp.exp2(sn - m)
            tot = jnp.sum(e, axis=-1, keepdims=True) + jnp.sum(en, axis=-1, keepdims=True)
            outs.append((_dot(e.astype(BF16), v) + _dot(en.astype(BF16), vn)) / tot)
        o = outs[0] - lam * outs[1]
        o_ref[:, lanes] = _subln(o, g_ref[...], post_scale).astype(o_ref.dtype)


def _attn_sample(q, q_row0, k, v, cache_k, cache_v, lam, subln, post_scale, bsz, t):
    d = q.shape[1]
    hw = 2 * HEAD
    nh = d // hw
    past = cache_k.shape[1]
    hg = _tile(nh, 4)
    assert (2 * hg) % 8 == 0 or hg == nh
    assert q_row0 % t == 0
    blk0 = q_row0 // t
    new = pl.BlockSpec((t, hg * hw), lambda b, g: (b, g))
    return pl.pallas_call(
        functools.partial(_attn_sample_kernel, hg=hg, post_scale=post_scale),
        out_shape=jax.ShapeDtypeStruct((bsz * t, d), BF16),
        grid=(bsz, nh // hg),
        in_specs=[pl.BlockSpec(memory_space=pltpu.SMEM),
                  pl.BlockSpec((t, hg * hw), lambda b, g: (blk0 + b, g)), new, new,
                  pl.BlockSpec((past, 2 * hg, HEAD), lambda b, g: (b, g, 0)),
                  pl.BlockSpec((past, hg * hw), lambda b, g: (b, g)),
                  pl.BlockSpec((1, hw), lambda b, g: (0, 0))],
        out_specs=pl.BlockSpec((t, hg * hw), lambda b, g: (b, g)),
        compiler_params=_params(("parallel", "parallel")),
        name="diff_attn_sample",
    )(lam.reshape(1), q, k, v, cache_k.reshape(bsz * past, 2 * nh, HEAD), cache_v.reshape(bsz * past, d),
      subln.reshape(1, hw))


def _expert_changed(be_ref):
    b = pl.program_id(1)
    return (b == 0) | (be_ref[b] != be_ref[jnp.maximum(b - 1, 0)])


def _pack_halves(x):
    half = x.shape[1] // 2
    as_bits = lambda v: lax.bitcast_convert_type(v.astype(BF16).astype(F32), jnp.uint32)
    return (as_bits(x[:, :half]) >> 16) | (as_bits(x[:, half:]) & jnp.uint32(0xFFFF0000))


def _unpack_halves(u):
    lo = lax.bitcast_convert_type(u << 16, F32).astype(BF16)
    hi = lax.bitcast_convert_type(u & jnp.uint32(0xFFFF0000), F32).astype(BF16)
    return lo, hi


def _moe_dispatch_kernel(tok_ref, nu_ref, x_hbm, o_ref, buf_ref, sem_ref, *, rows):
    i = pl.program_id(0)
    n_used = nu_ref[0]

    def copy(tile, r, slot):
        return pltpu.make_async_copy(x_hbm.at[tok_ref[tile * rows + r]], buf_ref.at[slot, r], sem_ref.at[slot])

    def start_tile(tile, slot):
        lax.fori_loop(0, rows, lambda r, c: (copy(tile, r, slot).start(), c)[1], 0, unroll=DMA_ISSUE_UNROLL)

    @pl.when(i == 0)
    def _():
        start_tile(0, 0)

    @pl.when(i + 1 < n_used)
    def _():
        start_tile(i + 1, (i + 1) % 2)

    @pl.when(i < n_used)
    def _():
        slot = i % 2
        lax.fori_loop(0, rows, lambda r, c: (copy(i, r, slot).wait(), c)[1], 0, unroll=DMA_ISSUE_UNROLL)
        o_ref[...] = buf_ref[slot]

    @pl.when(i >= n_used)
    def _():
        o_ref[...] = jnp.zeros(o_ref.shape, o_ref.dtype)


def _moe_dispatch(xq, slot_tok, n_used):
    w = xq.shape[1]
    p = slot_tok.shape[0]
    rows = MOE_ROWS
    return pl.pallas_call(
        functools.partial(_moe_dispatch_kernel, rows=rows),
        out_shape=jax.ShapeDtypeStruct((p, w), xq.dtype),
        grid_spec=pltpu.PrefetchScalarGridSpec(
            num_scalar_prefetch=2, grid=(p // rows,),
            in_specs=[pl.BlockSpec(memory_space=pl.ANY)],
            out_specs=pl.BlockSpec((rows, w), lambda i, tok, nu: (i, 0)),
            scratch_shapes=[pltpu.VMEM((2, rows, w), xq.dtype), pltpu.SemaphoreType.DMA((2,))]),
        compiler_params=_params(("arbitrary",)),
        name="moe_dispatch",
    )(slot_tok, n_used, xq)


def _moe_combine_kernel(pos_ref, h_ref, g_ref, gn_ref, yb_hbm, o_ref, on_ref, buf_ref, sem_ref, *, rows):
    i = pl.program_id(0)
    n_tiles = pl.num_programs(0)

    def copy(tile, r, k, slot):
        return pltpu.make_async_copy(yb_hbm.at[pos_ref[(tile * rows + r) * TOP_K + k]], buf_ref.at[slot, k, r],
                                     sem_ref.at[slot])

    def start_tile(tile, slot):
        def body(r, carry):
            for k in range(TOP_K):
                copy(tile, r, k, slot).start()
            return carry
        lax.fori_loop(0, rows, body, 0, unroll=DMA_ISSUE_UNROLL)

    @pl.when(i == 0)
    def _():
        start_tile(0, 0)

    @pl.when(i + 1 < n_tiles)
    def _():
        start_tile(i + 1, (i + 1) % 2)

    slot = i % 2

    def drain(r, carry):
        for k in range(TOP_K):
            copy(i, r, k, slot).wait()
        return carry

    lax.fori_loop(0, rows, drain, 0, unroll=DMA_ISSUE_UNROLL)
    g = g_ref[...]
    y = buf_ref[slot, 0] * g[:, 0:1]
    for k in range(1, TOP_K):
        y = y + buf_ref[slot, k] * g[:, k:k + 1]
    out = h_ref[...] + y
    o_ref[...] = out
    on = out * lax.rsqrt(jnp.mean(out * out, axis=-1, keepdims=True) + EPS)
    on_ref[...] = (on * gn_ref[...]).astype(on_ref.dtype)


def _moe_combine(h, gates, yb, pos, g_next):
    n, d = h.shape
    rows = _tile(n, 128)
    tile = pl.BlockSpec((rows, d), lambda i, pos: (i, 0))
    return pl.pallas_call(
        functools.partial(_moe_combine_kernel, rows=rows),
        out_shape=(jax.ShapeDtypeStruct((n, d), F32), jax.ShapeDtypeStruct((n, d), BF16)),
        grid_spec=pltpu.PrefetchScalarGridSpec(
            num_scalar_prefetch=1, grid=(n // rows,),
            in_specs=[tile, pl.BlockSpec((rows, TOP_K), lambda i, pos: (i, 0)),
                      pl.BlockSpec((1, d), lambda i, pos: (0, 0)), pl.BlockSpec(memory_space=pl.ANY)],
            out_specs=(tile, tile),
            scratch_shapes=[pltpu.VMEM((2, TOP_K, rows, d), F32), pltpu.SemaphoreType.DMA((2,))]),
        compiler_params=_params(("arbitrary",)),
        name="moe_combine",
    )(pos.reshape(-1), h, gates, g_next.reshape(1, d), yb)


def _moe_up_kernel(be_ref, we_ref, wj_ref, nu_ref, x_ref, wg_ref, wu_ref, o_ref, wgb_ref, wub_ref):
    del we_ref, wj_ref

    @pl.when(pl.program_id(1) < nu_ref[0])
    def _():
        @pl.when(_expert_changed(be_ref))
        def _():
            wgb_ref[...] = wg_ref[...].astype(BF16)
            wub_ref[...] = wu_ref[...].astype(BF16)

        xl, xh = _unpack_halves(x_ref[...])
        half = xl.shape[1]
        g = _dot(xl, wgb_ref[0:half, :]) + _dot(xh, wgb_ref[half:, :])
        u = _dot(xl, wub_ref[0:half, :]) + _dot(xh, wub_ref[half:, :])
        o_ref[...] = (jax.nn.silu(g) * u).astype(o_ref.dtype)

    @pl.when(pl.program_id(1) >= nu_ref[0])
    def _():
        o_ref[...] = jnp.zeros(o_ref.shape, o_ref.dtype)


def _moe_down_kernel(be_ref, we_ref, wj_ref, nu_ref, h_ref, wd_ref, o_ref, wdb_ref):
    del we_ref, wj_ref

    @pl.when(pl.program_id(1) < nu_ref[0])
    def _():
        @pl.when(_expert_changed(be_ref))
        def _():
            wdb_ref[...] = wd_ref[...].astype(BF16)

        o_ref[...] = _dot(h_ref[...], wdb_ref[...])

    @pl.when(pl.program_id(1) >= nu_ref[0])
    def _():
        o_ref[...] = jnp.zeros(o_ref.shape, o_ref.dtype)


def _moe_experts(xs, plan, layer, w_gate, w_up, w_down):
    blk_exp, w_exp, w_joff, n_used = plan
    p, w = xs.shape
    d = 2 * w
    de = w_gate.shape[3]
    nb = p // MOE_ROWS
    te = _tile(de, 512)
    row = lambda j, b, be, we, wj, nu: (jnp.minimum(b, nu[0] - 1), 0)
    out = lambda j, b, be, we, wj, nu: (b, j)

    def weight(n_j):
        return lambda j, b, be, we, wj, nu: (layer, we[b], 0, jnp.minimum(j + wj[b], n_j - 1))

    hidden = pl.pallas_call(
        _moe_up_kernel,
        out_shape=jax.ShapeDtypeStruct((p, de), BF16),
        grid_spec=pltpu.PrefetchScalarGridSpec(
            num_scalar_prefetch=4, grid=(de // te, nb),
            in_specs=[pl.BlockSpec((MOE_ROWS, w), row),
                      pl.BlockSpec((None, None, d, te), weight(de // te)),
                      pl.BlockSpec((None, None, d, te), weight(de // te))],
            out_specs=pl.BlockSpec((MOE_ROWS, te), out),
            scratch_shapes=[pltpu.VMEM((d, te), BF16), pltpu.VMEM((d, te), BF16)]),
        compiler_params=_params(("arbitrary", "arbitrary")),
        name="moe_gate_up",
    )(blk_exp, w_exp, w_joff, n_used, xs, w_gate, w_up)
    tn = _tile(d, 2048)
    return pl.pallas_call(
        _moe_down_kernel,
        out_shape=jax.ShapeDtypeStruct((p, d), F32),
        grid_spec=pltpu.PrefetchScalarGridSpec(
            num_scalar_prefetch=4, grid=(d // tn, nb),
            in_specs=[pl.BlockSpec((MOE_ROWS, de), row),
                      pl.BlockSpec((None, None, de, tn), weight(d // tn))],
            out_specs=pl.BlockSpec((MOE_ROWS, tn), out),
            scratch_shapes=[pltpu.VMEM((de, tn), BF16)]),
        compiler_params=_params(("arbitrary", "arbitrary")),
        name="moe_down",
    )(blk_exp, w_exp, w_joff, n_used, hidden, w_down)


def _slot_rank_kernel(e_ref, tri_ref, rank_ref, count_ref, carry_ref):
    i = pl.program_id(0)

    @pl.when(i == 0)
    def _():
        carry_ref[...] = jnp.zeros(carry_ref.shape, F32)

    onehot = e_ref[...] == lax.broadcasted_iota(jnp.int32, (e_ref.shape[0], ROUTER_LANES), 1)
    seen = _dot(tri_ref[...], jnp.where(onehot, 1.0, 0.0).astype(BF16)) + carry_ref[...]
    rank_ref[...] = (jnp.sum(jnp.where(onehot, seen, 0.0), axis=-1, keepdims=True) - 1.0).astype(jnp.int32)
    carry_ref[...] = seen[-1:, :]

    @pl.when(i == pl.num_programs(0) - 1)
    def _():
        count_ref[...] = seen[-1:, :].astype(jnp.int32)


def _slot_rank(flat_e):
    a = flat_e.shape[0]
    g = _tile(a, RANK_GROUP)
    rank, count = pl.pallas_call(
        _slot_rank_kernel,
        out_shape=(jax.ShapeDtypeStruct((a, 1), jnp.int32), jax.ShapeDtypeStruct((1, ROUTER_LANES), jnp.int32)),
        grid=(a // g,),
        in_specs=[pl.BlockSpec((g, 1), lambda i: (i, 0)), pl.BlockSpec((g, g), lambda i: (0, 0))],
        out_specs=(pl.BlockSpec((g, 1), lambda i: (i, 0)), pl.BlockSpec((1, ROUTER_LANES), lambda i: (0, 0))),
        scratch_shapes=[pltpu.VMEM((1, ROUTER_LANES), F32)],
        compiler_params=_params(("arbitrary",)),
        name="moe_slot_rank",
    )(flat_e.reshape(a, 1), jnp.asarray(np.tril(np.ones((g, g), np.float32)), BF16))
    return rank[:, 0], count[0, :N_EXPERTS]


def _dispatch(expert_idx):
    n = expert_idx.shape[0]
    a = n * TOP_K
    flat_e = expert_idx.reshape(-1).astype(jnp.int32)
    rank, counts = _slot_rank(flat_e)
    padded = (counts + MOE_ROWS - 1) // MOE_ROWS * MOE_ROWS
    pend = jnp.cumsum(padded)
    pstart = pend - padded
    n_blocks = -(-(a + N_EXPERTS * (MOE_ROWS - 1)) // MOE_ROWS)
    pos = pstart.at[flat_e].get(mode="promise_in_bounds") + rank
    slot_tok = jnp.zeros((n_blocks * MOE_ROWS,), jnp.int32).at[pos].set(
        jnp.arange(a, dtype=jnp.int32) // TOP_K, mode="promise_in_bounds", unique_indices=True)

    n_used = pend[-1] // MOE_ROWS
    blk = jnp.minimum(jnp.arange(n_blocks, dtype=jnp.int32), n_used - 1)
    expert_of = lambda b: jnp.minimum(jnp.sum((pend[None, :] <= (b * MOE_ROWS)[:, None]).astype(jnp.int32), axis=1),
                                      N_EXPERTS - 1)
    blk_exp = expert_of(blk)
    first = (blk == 0) | (blk_exp != expert_of(jnp.maximum(blk - 1, 0)))
    nxt = pend[blk_exp] // MOE_ROWS
    wraps = nxt >= n_used
    nxt_exp = jnp.where(wraps, expert_of(jnp.zeros_like(nxt)), expert_of(jnp.minimum(nxt, n_used - 1)))
    w_exp = jnp.where(first, blk_exp, nxt_exp)
    w_joff = jnp.where(first, 0, wraps.astype(jnp.int32))
    plan = (blk_exp, w_exp.astype(jnp.int32), w_joff.astype(jnp.int32), n_used.reshape(1).astype(jnp.int32))
    return slot_tok, pos.reshape(n, TOP_K), plan


def kernel(x_prompt, x_sample, state_hgrn, cache_k, cache_v, p_prompt, p_sample, norm_mix, norm_ffn, norm_ple, norm_final, a_w_in, a_w_out, a_lb_logits, a_gnorm, b_w_in, b_w_out, b_lambda_q1, b_lambda_k1, b_lambda_q2, b_lambda_k2, b_subln, router_group_w, router_group_b, router_expert_w, router_expert_b, expert_w_gate, expert_w_up, expert_w_down, ple_w_up, ple_w_gate):
    bp, tp, d = x_prompt.shape
    bs, ts, _ = x_sample.shape
    n_p, n_s = bp * tp, bs * ts
    depth = norm_mix.shape[0]
    nh_b = d // (2 * HEAD)

    h = (x_prompt.reshape(n_p, d), x_sample.reshape(n_s, d))
    lb_all = jnp.cumsum(jax.nn.softmax(a_lb_logits.astype(F32), axis=0), axis=0)
    states_p, states_s, k_p, v_p, k_s, v_s = [], [], [], [], [], []

    for i in range(depth):
        j = i // 2
        xn = _rmsnorm2(*h, norm_mix[i], BF16) if isinstance(h, tuple) else _rmsnorm(h, norm_mix[i], BF16)
        if i % 2 == 0:
            proj = _matmul(xn, a_w_in[j].astype(BF16))
            o_p, s_p = _hgrn2(proj, 0, bp, tp, lb_all[j], a_gnorm[j], None)
            o_s, s_s = _hgrn2(proj, n_p, bs, ts, lb_all[j], a_gnorm[j], state_hgrn[j])
            states_p.append(s_p)
            states_s.append(s_s)
            w_out = a_w_out[j]
        else:
            w_in = b_w_in[j]
            wk, wv = w_in[:, d:2 * d].astype(BF16), w_in[:, 2 * d:].astype(BF16)
            q = _matmul(xn, w_in[:, :d].astype(BF16), out_dtype=BF16, scale=HEAD ** -0.5 * LOG2E)
            (kp, kp16), vp = _matmul_rowsplit(xn, wk, rows=n_p), _matmul(xn, wv, rows=n_p)
            (ks, ks16), vs = _matmul_rowsplit(xn, wk, row0=n_p, rows=n_s), _matmul(xn, wv, row0=n_p, rows=n_s)
            lam0 = 0.8 - 0.6 * math.exp(-0.3 * i)
            lam = (jnp.exp(jnp.sum(b_lambda_q1[j] * b_lambda_k1[j]))
                   - jnp.exp(jnp.sum(b_lambda_q2[j] * b_lambda_k2[j])) + lam0).astype(F32)
            o_p = _attn_prompt(q, kp16, vp, lam, b_subln[j], 1.0 - lam0, bp, tp)
            o_s = _attn_sample(q, n_p, ks16, vs, cache_k[j], cache_v[j], lam, b_subln[j], 1.0 - lam0, bs, ts)
            k_p.append(kp.reshape(bp, tp, nh_b, 2, HEAD))
            v_p.append(vp.reshape(bp, tp, nh_b, 2 * HEAD))
            k_s.append(ks.reshape(bs, ts, nh_b, 2, HEAD))
            v_s.append(vs.reshape(bs, ts, nh_b, 2 * HEAD))
            w_out = b_w_out[j]
        h = _matmul_residual2(o_p, o_s, w_out.astype(BF16), h)

        w_router = jnp.zeros((d, ROUTER_LANES), F32)
        w_router = w_router.at[:, :N_GROUPS].set(router_group_w[i])
        w_router = w_router.at[:, N_GROUPS:N_GROUPS + N_EXPERTS].set(router_expert_w[i])
        b_router = jnp.zeros((ROUTER_LANES,), F32)
        b_router = b_router.at[:N_GROUPS].set(router_group_b[i].astype(F32))
        b_router = b_router.at[N_GROUPS:N_GROUPS + N_EXPERTS].set(router_expert_b[i].astype(F32))
        xq, expert_idx, gates = _rmsnorm_router(h, norm_ffn[i], w_router, b_router)
        slot_tok, pos, plan = _dispatch(expert_idx)
        yb = _moe_experts(_moe_dispatch(xq, slot_tok, plan[3]), plan, i, expert_w_gate, expert_w_up, expert_w_down)
        h, xn = _moe_combine(h, gates, yb, pos, norm_ple[i])

        p_i = jnp.concatenate([p_prompt[i].reshape(n_p, -1), p_sample[i].reshape(n_s, -1)], axis=0)
        h = _matmul_ple(xn, ple_w_gate[i].astype(BF16), h, p_i.astype(BF16), ple_w_up[i].astype(BF16))

    y_p = _rmsnorm(h, norm_final, F32, 0, n_p)
    y_s = _rmsnorm(h, norm_final, F32, n_p, n_s)
    return (y_p.reshape(bp, tp, d), y_s.reshape(bs, ts, d), jnp.stack(states_p), jnp.stack(states_s),
            jnp.stack(k_p), jnp.stack(v_p), jnp.stack(k_s), jnp.stack(v_s))
```

```python
import functools
import math

import jax
import jax.numpy as jnp
import numpy as np
from jax import lax
from jax.experimental import pallas as pl
from jax.experimental.pallas import tpu as pltpu

F32 = jnp.float32
BF16 = jnp.bfloat16

EPS = 1e-6
HEAD = 128
STREAM_CHUNK = 64
N_GROUPS = 4
EXPERTS_PER_GROUP = 8
N_EXPERTS = N_GROUPS * EXPERTS_PER_GROUP
TOP_K = 2
ROUTER_LANES = 128
MOE_ROWS = 256
RANK_GROUP = 512
HGRN2_HEADS_PER_STEP = 8
DMA_ISSUE_UNROLL = 8
ATTN_Q_ROWS = 512
V7X_VMEM_LIMIT = 56 * 1024 * 1024
LOG2E = math.log2(math.e)


def _tile(dim, want):
    t = min(dim, want)
    while dim % t:
        t //= 2
    return t


def _params(sem):
    return pltpu.CompilerParams(dimension_semantics=sem, vmem_limit_bytes=V7X_VMEM_LIMIT)


def _split3(x):
    hi = x.astype(BF16)
    r1 = x - hi.astype(F32)
    mid = r1.astype(BF16)
    lo = (r1 - mid.astype(F32)).astype(BF16)
    return hi, mid, lo


def _dot(a, b):
    return jnp.dot(a, b, preferred_element_type=F32)


def _dot_nt(a, b):
    return lax.dot_general(a, b, (((1,), (1,)), ((), ())), preferred_element_type=F32)


def _rmsnorm_kernel(x_ref, g_ref, o_ref):
    x = x_ref[...]
    y = x * lax.rsqrt(jnp.mean(x * x, axis=-1, keepdims=True) + EPS)
    o_ref[...] = (y * g_ref[...]).astype(o_ref.dtype)


def _rmsnorm(x, g, out_dtype, row0=0, rows=None):
    n, d = x.shape
    rows = n if rows is None else rows
    tm = _tile(math.gcd(rows, row0) if row0 else rows, 512)
    blk0 = row0 // tm
    return pl.pallas_call(
        _rmsnorm_kernel,
        out_shape=jax.ShapeDtypeStruct((rows, d), out_dtype),
        grid=(rows // tm,),
        in_specs=[pl.BlockSpec((tm, d), lambda i: (blk0 + i, 0)), pl.BlockSpec((1, d), lambda i: (0, 0))],
        out_specs=pl.BlockSpec((tm, d), lambda i: (i, 0)),
        compiler_params=_params(("parallel",)),
        name="rmsnorm",
    )(x, g.reshape(1, d))


def _rmsnorm2_kernel(xa_ref, xb_ref, g_ref, o_ref, *, a_tiles):
    i = pl.program_id(0)

    @pl.when(i < a_tiles)
    def _():
        _rmsnorm_kernel(xa_ref, g_ref, o_ref)

    @pl.when(i >= a_tiles)
    def _():
        _rmsnorm_kernel(xb_ref, g_ref, o_ref)


def _rmsnorm2(xa, xb, g, out_dtype):
    na, d = xa.shape
    nb = xb.shape[0]
    tm = _tile(math.gcd(na, nb), 512)
    a_tiles = na // tm
    return pl.pallas_call(
        functools.partial(_rmsnorm2_kernel, a_tiles=a_tiles),
        out_shape=jax.ShapeDtypeStruct((na + nb, d), out_dtype),
        grid=((na + nb) // tm,),
        in_specs=[pl.BlockSpec((tm, d), lambda i: (jnp.minimum(i, a_tiles - 1), 0)),
                  pl.BlockSpec((tm, d), lambda i: (jnp.maximum(i - a_tiles, 0), 0)),
                  pl.BlockSpec((1, d), lambda i: (0, 0))],
        out_specs=pl.BlockSpec((tm, d), lambda i: (i, 0)),
        compiler_params=_params(("parallel",)),
        name="rmsnorm",
    )(xa, xb, g.reshape(1, d))


def _first_lane(cond, lane):
    return jnp.min(jnp.where(cond, lane, float(ROUTER_LANES)), axis=-1, keepdims=True)


def _route_lanes(logits):
    lane = lax.broadcasted_iota(jnp.int32, logits.shape, 1).astype(F32)
    is_group = lane < N_GROUPS

    def softmax_over(mask):
        z = jnp.where(mask, logits, -jnp.inf)
        e = jnp.exp(z - jnp.max(z, axis=-1, keepdims=True))
        return e / jnp.sum(e, axis=-1, keepdims=True)

    pg = softmax_over(is_group)
    pg_top = jnp.max(pg, axis=-1, keepdims=True)
    g_top = _first_lane((pg == pg_top) & is_group, lane)
    lo = N_GROUPS + EXPERTS_PER_GROUP * g_top
    in_group = (lane >= lo) & (lane < lo + EXPERTS_PER_GROUP)
    pe = jnp.where(in_group, softmax_over(in_group), -1.0)
    p1 = jnp.max(pe, axis=-1, keepdims=True)
    e1 = _first_lane(pe == p1, lane)
    pe_rest = jnp.where(lane == e1, -1.0, pe)
    p2 = jnp.max(pe_rest, axis=-1, keepdims=True)
    e2 = _first_lane(pe_rest == p2, lane)
    scale = pg_top / (p1 + p2)
    out = jnp.where(lane == 0, p1 * scale, jnp.where(lane == 1, p2 * scale, 0.0))
    return jnp.where(lane == 2, e1 - N_GROUPS, jnp.where(lane == 3, e2 - N_GROUPS, out))


def _rmsnorm_router_kernel(x_ref, g_ref, w0_ref, w1_ref, w2_ref, b_ref, o_ref, r_ref):
    x = x_ref[...]
    y = x * lax.rsqrt(jnp.mean(x * x, axis=-1, keepdims=True) + EPS)
    xn = y * g_ref[...]
    o_ref[...] = _pack_halves(xn)
    x0, x1, x2 = _split3(xn)
    w0, w1, w2 = w0_ref[...], w1_ref[...], w2_ref[...]
    small = _dot(x0, w2) + _dot(x1, w1) + _dot(x2, w0)
    mid = _dot(x0, w1) + _dot(x1, w0)
    r_ref[...] = _route_lanes(_dot(x0, w0) + (mid + small) + b_ref[...])


def _rmsnorm_router(x, g, w_router, b_router):
    n, d = x.shape
    tm = _tile(n, 512)
    w0, w1, w2 = _split3(w_router)
    wspec = pl.BlockSpec((d, ROUTER_LANES), lambda i: (0, 0))
    xq, route = pl.pallas_call(
        _rmsnorm_router_kernel,
        out_shape=(jax.ShapeDtypeStruct((n, d // 2), jnp.uint32), jax.ShapeDtypeStruct((n, ROUTER_LANES), F32)),
        grid=(n // tm,),
        in_specs=[pl.BlockSpec((tm, d), lambda i: (i, 0)), pl.BlockSpec((1, d), lambda i: (0, 0)),
                  wspec, wspec, wspec, pl.BlockSpec((1, ROUTER_LANES), lambda i: (0, 0))],
        out_specs=(pl.BlockSpec((tm, d // 2), lambda i: (i, 0)), pl.BlockSpec((tm, ROUTER_LANES), lambda i: (i, 0))),
        compiler_params=_params(("parallel",)),
        name="rmsnorm_router",
    )(x, g.reshape(1, d), w0, w1, w2, b_router.reshape(1, ROUTER_LANES))
    return xq, route[:, TOP_K:2 * TOP_K].astype(jnp.int32), route[:, :TOP_K]


def _mm_kernel(x_ref, w_ref, o_ref, *, scale):
    acc = _dot(x_ref[...], w_ref[...])
    if scale is not None:
        acc = acc * scale
    o_ref[...] = acc.astype(o_ref.dtype)


def _mm_rowsplit_kernel(x_ref, w_ref, o_ref, o16_ref):
    acc = _dot(x_ref[...], w_ref[...])
    o16_ref[...] = acc.astype(o16_ref.dtype)
    for c in range(o_ref.shape[1]):
        o_ref[:, c, :] = acc[:, c * HEAD:(c + 1) * HEAD]


def _matmul_rowsplit(x, w, *, row0=0, rows=None, tm=1024, tn=1024):
    m, k = x.shape
    rows = m if rows is None else rows
    n = w.shape[1]
    tm = _tile(math.gcd(rows, row0) if row0 else rows, tm)
    tn = _tile(n, tn)
    blk0 = row0 // tm
    return pl.pallas_call(
        _mm_rowsplit_kernel,
        out_shape=(jax.ShapeDtypeStruct((rows, n // HEAD, HEAD), F32), jax.ShapeDtypeStruct((rows, n), BF16)),
        grid=(rows // tm, n // tn),
        in_specs=[pl.BlockSpec((tm, k), lambda i, j: (blk0 + i, 0)), pl.BlockSpec((k, tn), lambda i, j: (0, j))],
        out_specs=(pl.BlockSpec((tm, tn // HEAD, HEAD), lambda i, j: (i, j, 0)),
                   pl.BlockSpec((tm, tn), lambda i, j: (i, j))),
        compiler_params=_params(("parallel", "arbitrary")),
        name="matmul_rowsplit",
    )(x, w)


def _mm_res2_kernel(xa_ref, xb_ref, w_ref, ra_ref, rb_ref, o_ref, *, a_tiles):
    i = pl.program_id(0)

    @pl.when(i < a_tiles)
    def _():
        o_ref[...] = ra_ref[...] + _dot(xa_ref[...], w_ref[...])

    @pl.when(i >= a_tiles)
    def _():
        o_ref[...] = rb_ref[...] + _dot(xb_ref[...], w_ref[...])


def _mm_ple_kernel(x_ref, w_ref, r_ref, p_ref, wup_ref, o_ref):
    gate = jax.nn.sigmoid(_dot(x_ref[...], w_ref[...]))
    o_ref[...] = r_ref[...] + _dot(p_ref[...], wup_ref[...]) * gate


def _matmul(x, w, *, out_dtype=F32, scale=None, row0=0, rows=None, tm=1024, tn=1024):
    m, k = x.shape
    rows = m if rows is None else rows
    n = w.shape[1]
    tm = _tile(math.gcd(rows, row0) if row0 else rows, tm)
    tn = _tile(n, tn)
    blk0 = row0 // tm
    return pl.pallas_call(
        functools.partial(_mm_kernel, scale=scale),
        out_shape=jax.ShapeDtypeStruct((rows, n), out_dtype),
        grid=(rows // tm, n // tn),
        in_specs=[pl.BlockSpec((tm, k), lambda i, j: (blk0 + i, 0)), pl.BlockSpec((k, tn), lambda i, j: (0, j))],
        out_specs=pl.BlockSpec((tm, tn), lambda i, j: (i, j)),
        compiler_params=_params(("parallel", "arbitrary")),
        name="matmul",
    )(x, w)


def _matmul_residual2(xa, xb, w, res):
    ma, k = xa.shape
    mb = xb.shape[0]
    n = w.shape[1]
    tm, tn = _tile(math.gcd(ma, mb), 1024), _tile(n, 512)
    a_tiles, b_tiles = ma // tm, mb // tm
    in_a = lambda i: jnp.minimum(i, a_tiles - 1)
    in_b = lambda i: jnp.maximum(i - a_tiles, 0)
    if isinstance(res, tuple):
        b_off = 0
    else:
        res, b_off = (res, res), a_tiles
    return pl.pallas_call(
        functools.partial(_mm_res2_kernel, a_tiles=a_tiles),
        out_shape=jax.ShapeDtypeStruct((ma + mb, n), F32),
        grid=(a_tiles + b_tiles, n // tn),
        in_specs=[pl.BlockSpec((tm, k), lambda i, j: (in_a(i), 0)),
                  pl.BlockSpec((tm, k), lambda i, j: (in_b(i), 0)),
                  pl.BlockSpec((k, tn), lambda i, j: (0, j)),
                  pl.BlockSpec((tm, tn), lambda i, j: (in_a(i), jnp.where(i < a_tiles, j, n // tn - 1))),
                  pl.BlockSpec((tm, tn), lambda i, j: (b_off + in_b(i), jnp.where(i < a_tiles, 0, j)))],
        out_specs=pl.BlockSpec((tm, tn), lambda i, j: (i, j)),
        compiler_params=_params(("parallel", "arbitrary")),
        name="matmul_residual",
    )(xa, xb, w, *res)


def _matmul_ple(xn, w_gate, res, p, w_up, *, tm=1024, tn=512):
    m, k = xn.shape
    n = w_gate.shape[1]
    kp = p.shape[1]
    tm, tn = _tile(m, tm), _tile(n, tn)
    return pl.pallas_call(
        _mm_ple_kernel,
        out_shape=jax.ShapeDtypeStruct((m, n), F32),
        grid=(m // tm, n // tn),
        in_specs=[pl.BlockSpec((tm, k), lambda i, j: (i, 0)), pl.BlockSpec((k, tn), lambda i, j: (0, j)),
                  pl.BlockSpec((tm, tn), lambda i, j: (i, j)),
                  pl.BlockSpec((tm, kp), lambda i, j: (i, 0)), pl.BlockSpec((kp, tn), lambda i, j: (0, j))],
        out_specs=pl.BlockSpec((tm, tn), lambda i, j: (i, j)),
        compiler_params=_params(("parallel", "arbitrary")),
        name="matmul_ple",
    )(xn, w_gate, res, p, w_up)


def _hgrn2_level_table(c):
    t = np.arange(c)[:, None]
    s = np.arange(c)[None, :]
    x = np.bitwise_xor(t, s)
    lvl = np.where(x > 0, np.floor(np.log2(np.maximum(x, 1))).astype(np.int32), -1)
    return np.where(s < t, lvl, -1).astype(np.int32)


def _hgrn2_sign_table(c):
    t = np.arange(c)[None, :, None]
    j = np.arange(int(math.log2(c)))[:, None, None]
    return np.broadcast_to(np.where((t >> j) & 1, 1.0, -1.0), (j.shape[0], c, HEAD)).astype(np.float32)


def _hgrn2_kernel(q_ref, f_ref, v_ref, g_ref, lb_ref, gn_ref, lvl_ref, tri_ref, sgn_ref, *rest, chunk, n_chunks, hp,
                  has_s0):
    if has_s0:
        s0_ref, o_ref, s_ref, st_ref, b_ref = rest
    else:
        o_ref, s_ref, st_ref, b_ref = rest
    tb = pl.program_id(2)
    c = chunk

    @pl.when(tb == 0)
    def _():
        for hh in range(hp):
            st_ref[hh] = s0_ref[0, hh].T if has_s0 else jnp.zeros((HEAD, HEAD), F32)

    gn = gn_ref[...]
    lvl = lvl_ref[...]
    tri = tri_ref[...]
    sub = lax.broadcasted_iota(jnp.int32, (8, HEAD), 0)
    n_levels = int(math.log2(c))

    def gates(rows, hh):
        lanes = slice(hh * HEAD, (hh + 1) * HEAD)
        lb = lb_ref[:, lanes]
        q = jax.nn.silu(q_ref[rows, lanes])
        f = lb + (1.0 - lb) * jax.nn.sigmoid(f_ref[rows, lanes])
        k = 1.0 - f
        lg = jnp.log(f) * LOG2E
        l0, l1, l2 = _split3(lg)
        cs = _dot(tri, jnp.concatenate([l0, l1, l2], axis=1))
        b = cs[:, :HEAD] + (cs[:, HEAD:2 * HEAD] + cs[:, 2 * HEAD:])
        b_ref[hh] = b
        att = jnp.where(lvl == 0, _dot_nt((q * f).astype(BF16), k.astype(BF16)), 0.0)
        return q, k, b, att

    def level(j, hh, q, k, b, att):
        m = 1 << j
        pieces = []
        for g8 in range(c // 8):
            if m >= 4:
                r = (g8 * 8 // (2 * m)) * 2 * m + m - 1
                pieces.append(jnp.broadcast_to(b_ref[hh, r:r + 1, :], (8, HEAD)))
            else:
                top = jnp.broadcast_to(b_ref[hh, g8 * 8 + 1:g8 * 8 + 2, :], (8, HEAD))
                bot = jnp.broadcast_to(b_ref[hh, g8 * 8 + 5:g8 * 8 + 6, :], (8, HEAD))
                pieces.append(jnp.where(sub < 4, top, bot))
        e = jnp.exp2((b - jnp.concatenate(pieces, axis=0)) * sgn_ref[j])
        sj = _dot_nt((q * e).astype(BF16), (k * e).astype(BF16))
        return jnp.where(lvl == j, sj, att)

    def finish(rows, hh, q, k, b, att):
        lanes = slice(hh * HEAD, (hh + 1) * HEAD)
        v = v_ref[rows, lanes]
        st = st_ref[hh]
        b_end = b_ref[hh, c - 1:c, :]
        o = _dot(att.astype(BF16), v.astype(BF16))
        o = o + _dot_nt((q * jnp.exp2(b)).astype(BF16), st.astype(BF16))
        o = o + jnp.sum(q * k, axis=-1, keepdims=True) * v
        kd = (k * jnp.exp2(b_end - b)).astype(BF16)
        st_ref[hh] = jnp.exp2(b_end) * st + _dot(v.T.astype(BF16), kd)
        o = o * lax.rsqrt(jnp.mean(o * o, axis=-1, keepdims=True) + EPS) * gn
        o_ref[rows, lanes] = (o * jax.nn.silu(g_ref[rows, lanes])).astype(o_ref.dtype)

    def one_chunk(ci, carry):
        rows = pl.ds(pl.multiple_of(ci * c, c), c)
        work = [gates(rows, hh) for hh in range(hp)]
        for j in range(1, n_levels):
            work = [w[:3] + (level(j, hh, *w),) for hh, w in enumerate(work)]
        for hh, w in enumerate(work):
            finish(rows, hh, *w)
        return carry

    lax.fori_loop(0, n_chunks, one_chunk, 0)

    @pl.when(tb == pl.num_programs(2) - 1)
    def _():
        for hh in range(hp):
            s_ref[0, hh] = st_ref[hh].T


def _hgrn2(proj, row_off, bsz, t, lb, gnorm, s0):
    d = proj.shape[1] // 4
    nh = d // HEAD
    hp = _tile(nh, HGRN2_HEADS_PER_STEP)
    c = min(t, 128)
    n_levels = int(math.log2(c))
    tb = _tile(t, 512)
    n_tb = t // tb
    assert row_off % tb == 0 and tb % c == 0
    blk0 = row_off // tb
    ng = nh // hp
    col = lambda kind: pl.BlockSpec((tb, hp * HEAD), lambda b, h, i: (blk0 + b * n_tb + i, kind * ng + h))
    const = lambda shape: pl.BlockSpec(shape, lambda b, h, i: (0, 0))
    st_spec = pl.BlockSpec((1, hp, HEAD, HEAD), lambda b, h, i: (b, h, 0, 0))
    in_specs = [col(0), col(1), col(2), col(3), pl.BlockSpec((1, hp * HEAD), lambda b, h, i: (0, h)),
                const((1, HEAD)), const((c, c)), const((c, c)),
                pl.BlockSpec((n_levels, c, HEAD), lambda b, h, i: (0, 0, 0))]
    args = [proj, proj, proj, proj, lb.reshape(1, d), gnorm.reshape(1, HEAD),
            jnp.asarray(_hgrn2_level_table(c)), jnp.asarray(np.tril(np.ones((c, c), np.float32)), BF16),
            jnp.asarray(_hgrn2_sign_table(c))]
    if s0 is not None:
        in_specs.append(st_spec)
        args.append(s0)
    return pl.pallas_call(
        functools.partial(_hgrn2_kernel, chunk=c, n_chunks=tb // c, hp=hp, has_s0=s0 is not None),
        out_shape=(jax.ShapeDtypeStruct((bsz * t, d), BF16), jax.ShapeDtypeStruct((bsz, nh, HEAD, HEAD), F32)),
        grid=(bsz, ng, n_tb),
        in_specs=in_specs,
        out_specs=(pl.BlockSpec((tb, hp * HEAD), lambda b, h, i: (b * n_tb + i, h)), st_spec),
        scratch_shapes=[pltpu.VMEM((hp, HEAD, HEAD), F32), pltpu.VMEM((hp, c, HEAD), F32)],
        compiler_params=_params(("parallel", "parallel", "arbitrary")),
        name="hgrn2_scan",
    )(*args)


def _subln(o, g, post_scale):
    return o * lax.rsqrt(jnp.mean(o * o, axis=-1, keepdims=True) + EPS) * g * post_scale


def _attn_prompt_kernel(lam_ref, q_ref, kb_ref, v_ref, g_ref, o_ref, vb_ref, *, tq, n_q, post_scale):
    qi = pl.program_id(2)

    @pl.when(qi == 0)
    def _():
        vb_ref[...] = v_ref[...].astype(BF16)

    lam = lam_ref[0]
    q = q_ref[...]
    for vi in range(n_q):
        @pl.when(qi == vi)
        def _(head=vi * tq, ext=(vi + 1) * tq):
            pos = lax.broadcasted_iota(jnp.int32, (tq, tq), 0)
            allowed = lax.broadcasted_iota(jnp.int32, (tq, tq), 1) <= (pos | (STREAM_CHUNK - 1))
            v_tail = vb_ref[head:ext, :]

            def one_map(c0):
                qc = q[:, c0:c0 + HEAD]
                st = jnp.where(allowed, _dot_nt(qc, kb_ref[head:ext, c0:c0 + HEAD]), -jnp.inf)
                m = jnp.max(st, axis=-1, keepdims=True)
                if head:
                    sh = _dot_nt(qc, kb_ref[0:head, c0:c0 + HEAD])
                    m = jnp.maximum(m, jnp.max(sh, axis=-1, keepdims=True))
                et = jnp.exp2(st - m)
                tot = jnp.sum(et, axis=-1, keepdims=True)
                acc = _dot(et.astype(BF16), v_tail)
                if head:
                    eh = jnp.exp2(sh - m)
                    tot = tot + jnp.sum(eh, axis=-1, keepdims=True)
                    acc = acc + _dot(eh.astype(BF16), vb_ref[0:head, :])
                return acc / tot

            o = one_map(0) - lam * one_map(HEAD)
            o_ref[...] = _subln(o, g_ref[...], post_scale).astype(o_ref.dtype)


def _attn_prompt(q, k, v, lam, subln, post_scale, bsz, t):
    d = q.shape[1]
    hw = 2 * HEAD
    nh = d // hw
    tq = _tile(t, ATTN_Q_ROWS)
    n_q = t // tq
    return pl.pallas_call(
        functools.partial(_attn_prompt_kernel, tq=tq, n_q=n_q, post_scale=post_scale),
        out_shape=jax.ShapeDtypeStruct((bsz * t, d), BF16),
        grid=(bsz, nh, n_q),
        in_specs=[pl.BlockSpec(memory_space=pltpu.SMEM),
                  pl.BlockSpec((tq, hw), lambda b, h, i: (b * n_q + i, h)),
                  pl.BlockSpec((t, hw), lambda b, h, i: (b, h)),
                  pl.BlockSpec((t, hw), lambda b, h, i: (b, h)),
                  pl.BlockSpec((1, hw), lambda b, h, i: (0, 0))],
        out_specs=pl.BlockSpec((tq, hw), lambda b, h, i: (b * n_q + i, h)),
        scratch_shapes=[pltpu.VMEM((t, hw), BF16)],
        compiler_params=_params(("parallel", "parallel", "arbitrary")),
        name="diff_attn_prompt",
    )(lam.reshape(1), q, k, v, subln.reshape(1, hw))


def _attn_sample_kernel(lam_ref, q_ref, kn_ref, vn_ref, ck_ref, cv_ref, g_ref, o_ref, *, hg, post_scale):
    hw = 2 * HEAD
    lam = lam_ref[0]
    past = ck_ref.shape[0]
    ck_rows = ck_ref.reshape(past * 2 * hg, HEAD)
    for h in range(hg):
        lanes = slice(h * hw, (h + 1) * hw)
        v = cv_ref[:, lanes].astype(BF16)
        vn = vn_ref[:, lanes].astype(BF16)
        outs = []
        for c in range(2):
            col = slice((2 * h + c) * HEAD, (2 * h + c + 1) * HEAD)
            qc = q_ref[:, col]
            s = _dot_nt(qc, ck_rows[pl.ds(2 * h + c, past, stride=2 * hg), :].astype(BF16))
            sn = _dot_nt(qc, kn_ref[:, col])
            m = jnp.maximum(jnp.max(s, axis=-1, keepdims=True), jnp.max(sn, axis=-1, keepdims=True))
            e, en = jnp.exp2(s - m), jnp.exp2(sn - m)
            tot = jnp.sum(e, axis=-1, keepdims=True) + jnp.sum(en, axis=-1, keepdims=True)
            outs.append((_dot(e.astype(BF16), v) + _dot(en.astype(BF16), vn)) / tot)
        o = outs[0] - lam * outs[1]
        o_ref[:, lanes] = _subln(o, g_ref[...], post_scale).astype(o_ref.dtype)


def _attn_sample(q, q_row0, k, v, cache_k, cache_v, lam, subln, post_scale, bsz, t):
    d = q.shape[1]
    hw = 2 * HEAD
    nh = d // hw
    past = cache_k.shape[1]
    hg = _tile(nh, 4)
    assert (2 * hg) % 8 == 0 or hg == nh
    assert q_row0 % t == 0
    blk0 = q_row0 // t
    new = pl.BlockSpec((t, hg * hw), lambda b, g: (b, g))
    return pl.pallas_call(
        functools.partial(_attn_sample_kernel, hg=hg, post_scale=post_scale),
        out_shape=jax.ShapeDtypeStruct((bsz * t, d), BF16),
        grid=(bsz, nh // hg),
        in_specs=[pl.BlockSpec(memory_space=pltpu.SMEM),
                  pl.BlockSpec((t, hg * hw), lambda b, g: (blk0 + b, g)), new, new,
                  pl.BlockSpec((past, 2 * hg, HEAD), lambda b, g: (b, g, 0)),
                  pl.BlockSpec((past, hg * hw), lambda b, g: (b, g)),
                  pl.BlockSpec((1, hw), lambda b, g: (0, 0))],
        out_specs=pl.BlockSpec((t, hg * hw), lambda b, g: (b, g)),
        compiler_params=_params(("parallel", "parallel")),
        name="diff_attn_sample",
    )(lam.reshape(1), q, k, v, cache_k.reshape(bsz * past, 2 * nh, HEAD), cache_v.reshape(bsz * past, d),
      subln.reshape(1, hw))


def _expert_changed(be_ref):
    b = pl.program_id(1)
    return (b == 0) | (be_ref[b] != be_ref[jnp.maximum(b - 1, 0)])


def _pack_halves(x):
    half = x.shape[1] // 2
    as_bits = lambda v: lax.bitcast_convert_type(v.astype(BF16).astype(F32), jnp.uint32)
    return (as_bits(x[:, :half]) >> 16) | (as_bits(x[:, half:]) & jnp.uint32(0xFFFF0000))


def _unpack_halves(u):
    lo = lax.bitcast_convert_type(u << 16, F32).astype(BF16)
    hi = lax.bitcast_convert_type(u & jnp.uint32(0xFFFF0000), F32).astype(BF16)
    return lo, hi


def _moe_dispatch_kernel(tok_ref, nu_ref, x_hbm, o_ref, buf_ref, sem_ref, *, rows):
    i = pl.program_id(0)
    n_used = nu_ref[0]

    def copy(tile, r, slot):
        return pltpu.make_async_copy(x_hbm.at[tok_ref[tile * rows + r]], buf_ref.at[slot, r], sem_ref.at[slot])

    def start_tile(tile, slot):
        lax.fori_loop(0, rows, lambda r, c: (copy(tile, r, slot).start(), c)[1], 0, unroll=DMA_ISSUE_UNROLL)

    @pl.when(i == 0)
    def _():
        start_tile(0, 0)

    @pl.when(i + 1 < n_used)
    def _():
        start_tile(i + 1, (i + 1) % 2)

    @pl.when(i < n_used)
    def _():
        slot = i % 2
        pltpu.make_async_copy(x_hbm.at[pl.ds(0, rows)], buf_ref.at[slot], sem_ref.at[slot]).wait()
        o_ref[...] = buf_ref[slot]

    @pl.when(i >= n_used)
    def _():
        o_ref[...] = jnp.zeros(o_ref.shape, o_ref.dtype)


def _moe_dispatch(xq, slot_tok, n_used):
    w = xq.shape[1]
    p = slot_tok.shape[0]
    rows = MOE_ROWS
    return pl.pallas_call(
        functools.partial(_moe_dispatch_kernel, rows=rows),
        out_shape=jax.ShapeDtypeStruct((p, w), xq.dtype),
        grid_spec=pltpu.PrefetchScalarGridSpec(
            num_scalar_prefetch=2, grid=(p // rows,),
            in_specs=[pl.BlockSpec(memory_space=pl.ANY)],
            out_specs=pl.BlockSpec((rows, w), lambda i, tok, nu: (i, 0)),
            scratch_shapes=[pltpu.VMEM((2, rows, w), xq.dtype), pltpu.SemaphoreType.DMA((2,))]),
        compiler_params=_params(("arbitrary",)),
        name="moe_dispatch",
    )(slot_tok, n_used, xq)


def _moe_combine_kernel(pos_ref, h_ref, g_ref, gn_ref, yb_hbm, o_ref, on_ref, buf_ref, sem_ref, *, rows):
    i = pl.program_id(0)
    n_tiles = pl.num_programs(0)

    def copy(tile, r, k, slot):
        return pltpu.make_async_copy(yb_hbm.at[pos_ref[(tile * rows + r) * TOP_K + k]], buf_ref.at[slot, k, r],
                                     sem_ref.at[slot])

    def start_tile(tile, slot):
        def body(r, carry):
            for k in range(TOP_K):
                copy(tile, r, k, slot).start()
            return carry
        lax.fori_loop(0, rows, body, 0, unroll=DMA_ISSUE_UNROLL)

    @pl.when(i == 0)
    def _():
        start_tile(0, 0)

    @pl.when(i + 1 < n_tiles)
    def _():
        start_tile(i + 1, (i + 1) % 2)

    slot = i % 2
    for k in range(TOP_K):
        pltpu.make_async_copy(yb_hbm.at[pl.ds(0, rows)], buf_ref.at[slot, k], sem_ref.at[slot]).wait()
    g = g_ref[...]
    y = buf_ref[slot, 0] * g[:, 0:1]
    for k in range(1, TOP_K):
        y = y + buf_ref[slot, k] * g[:, k:k + 1]
    out = h_ref[...] + y
    o_ref[...] = out
    on = out * lax.rsqrt(jnp.mean(out * out, axis=-1, keepdims=True) + EPS)
    on_ref[...] = (on * gn_ref[...]).astype(on_ref.dtype)


def _moe_combine(h, gates, yb, pos, g_next):
    n, d = h.shape
    rows = _tile(n, 128)
    tile = pl.BlockSpec((rows, d), lambda i, pos: (i, 0))
    return pl.pallas_call(
        functools.partial(_moe_combine_kernel, rows=rows),
        out_shape=(jax.ShapeDtypeStruct((n, d), F32), jax.ShapeDtypeStruct((n, d), BF16)),
        grid_spec=pltpu.PrefetchScalarGridSpec(
            num_scalar_prefetch=1, grid=(n // rows,),
            in_specs=[tile, pl.BlockSpec((rows, TOP_K), lambda i, pos: (i, 0)),
                      pl.BlockSpec((1, d), lambda i, pos: (0, 0)), pl.BlockSpec(memory_space=pl.ANY)],
            out_specs=(tile, tile),
            scratch_shapes=[pltpu.VMEM((2, TOP_K, rows, d), F32), pltpu.SemaphoreType.DMA((2,))]),
        compiler_params=_params(("arbitrary",)),
        name="moe_combine",
    )(pos.reshape(-1), h, gates, g_next.reshape(1, d), yb)


def _moe_up_kernel(be_ref, we_ref, wj_ref, nu_ref, x_ref, wg_ref, wu_ref, o_ref, wgb_ref, wub_ref):
    del we_ref, wj_ref

    @pl.when(pl.program_id(1) < nu_ref[0])
    def _():
        @pl.when(_expert_changed(be_ref))
        def _():
            wgb_ref[...] = wg_ref[...].astype(BF16)
            wub_ref[...] = wu_ref[...].astype(BF16)

        xl, xh = _unpack_halves(x_ref[...])
        half = xl.shape[1]
        g = _dot(xl, wgb_ref[0:half, :]) + _dot(xh, wgb_ref[half:, :])
        u = _dot(xl, wub_ref[0:half, :]) + _dot(xh, wub_ref[half:, :])
        o_ref[...] = (jax.nn.silu(g) * u).astype(o_ref.dtype)

    @pl.when(pl.program_id(1) >= nu_ref[0])
    def _():
        o_ref[...] = jnp.zeros(o_ref.shape, o_ref.dtype)


def _moe_down_kernel(be_ref, we_ref, wj_ref, nu_ref, h_ref, wd_ref, o_ref, wdb_ref):
    del we_ref, wj_ref

    @pl.when(pl.program_id(1) < nu_ref[0])
    def _():
        @pl.when(_expert_changed(be_ref))
        def _():
            wdb_ref[...] = wd_ref[...].astype(BF16)

        o_ref[...] = _dot(h_ref[...], wdb_ref[...])

    @pl.when(pl.program_id(1) >= nu_ref[0])
    def _():
        o_ref[...] = jnp.zeros(o_ref.shape, o_ref.dtype)


def _moe_experts(xs, plan, layer, w_gate, w_up, w_down):
    blk_exp, w_exp, w_joff, n_used = plan
    p, w = xs.shape
    d = 2 * w
    de = w_gate.shape[3]
    nb = p // MOE_ROWS
    te = _tile(de, 512)
    row = lambda j, b, be, we, wj, nu: (jnp.minimum(b, nu[0] - 1), 0)
    out = lambda j, b, be, we, wj, nu: (b, j)

    def weight(n_j):
        return lambda j, b, be, we, wj, nu: (layer, we[b], 0, jnp.minimum(j + wj[b], n_j - 1))

    hidden = pl.pallas_call(
        _moe_up_kernel,
        out_shape=jax.ShapeDtypeStruct((p, de), BF16),
        grid_spec=pltpu.PrefetchScalarGridSpec(
            num_scalar_prefetch=4, grid=(de // te, nb),
            in_specs=[pl.BlockSpec((MOE_ROWS, w), row),
                      pl.BlockSpec((None, None, d, te), weight(de // te)),
                      pl.BlockSpec((None, None, d, te), weight(de // te))],
            out_specs=pl.BlockSpec((MOE_ROWS, te), out),
            scratch_shapes=[pltpu.VMEM((d, te), BF16), pltpu.VMEM((d, te), BF16)]),
        compiler_params=_params(("arbitrary", "arbitrary")),
        name="moe_gate_up",
    )(blk_exp, w_exp, w_joff, n_used, xs, w_gate, w_up)
    tn = _tile(d, 2048)
    return pl.pallas_call(
        _moe_down_kernel,
        out_shape=jax.ShapeDtypeStruct((p, d), F32),
        grid_spec=pltpu.PrefetchScalarGridSpec(
            num_scalar_prefetch=4, grid=(d // tn, nb),
            in_specs=[pl.BlockSpec((MOE_ROWS, de), row),
                      pl.BlockSpec((None, None, de, tn), weight(d // tn))],
            out_specs=pl.BlockSpec((MOE_ROWS, tn), out),
            scratch_shapes=[pltpu.VMEM((de, tn), BF16)]),
        compiler_params=_params(("arbitrary", "arbitrary")),
        name="moe_down",
    )(blk_exp, w_exp, w_joff, n_used, hidden, w_down)


def _slot_rank_kernel(e_ref, tri_ref, rank_ref, count_ref, carry_ref):
    i = pl.program_id(0)

    @pl.when(i == 0)
    def _():
        carry_ref[...] = jnp.zeros(carry_ref.shape, F32)

    onehot = e_ref[...] == lax.broadcasted_iota(jnp.int32, (e_ref.shape[0], ROUTER_LANES), 1)
    seen = _dot(tri_ref[...], jnp.where(onehot, 1.0, 0.0).astype(BF16)) + carry_ref[...]
    rank_ref[...] = (jnp.sum(jnp.where(onehot, seen, 0.0), axis=-1, keepdims=True) - 1.0).astype(jnp.int32)
    carry_ref[...] = seen[-1:, :]

    @pl.when(i == pl.num_programs(0) - 1)
    def _():
        count_ref[...] = seen[-1:, :].astype(jnp.int32)


def _slot_rank(flat_e):
    a = flat_e.shape[0]
    g = _tile(a, RANK_GROUP)
    rank, count = pl.pallas_call(
        _slot_rank_kernel,
        out_shape=(jax.ShapeDtypeStruct((a, 1), jnp.int32), jax.ShapeDtypeStruct((1, ROUTER_LANES), jnp.int32)),
        grid=(a // g,),
        in_specs=[pl.BlockSpec((g, 1), lambda i: (i, 0)), pl.BlockSpec((g, g), lambda i: (0, 0))],
        out_specs=(pl.BlockSpec((g, 1), lambda i: (i, 0)), pl.BlockSpec((1, ROUTER_LANES), lambda i: (0, 0))),
        scratch_shapes=[pltpu.VMEM((1, ROUTER_LANES), F32)],
        compiler_params=_params(("arbitrary",)),
        name="moe_slot_rank",
    )(flat_e.reshape(a, 1), jnp.asarray(np.tril(np.ones((g, g), np.float32)), BF16))
    return rank[:, 0], count[0, :N_EXPERTS]


def _dispatch(expert_idx):
    n = expert_idx.shape[0]
    a = n * TOP_K
    flat_e = expert_idx.reshape(-1).astype(jnp.int32)
    rank, counts = _slot_rank(flat_e)
    padded = (counts + MOE_ROWS - 1) // MOE_ROWS * MOE_ROWS
    pend = jnp.cumsum(padded)
    pstart = pend - padded
    n_blocks = -(-(a + N_EXPERTS * (MOE_ROWS - 1)) // MOE_ROWS)
    pos = pstart.at[flat_e].get(mode="promise_in_bounds") + rank
    slot_tok = jnp.zeros((n_blocks * MOE_ROWS,), jnp.int32).at[pos].set(
        jnp.arange(a, dtype=jnp.int32) // TOP_K, mode="promise_in_bounds", unique_indices=True)

    n_used = pend[-1] // MOE_ROWS
    blk = jnp.minimum(jnp.arange(n_blocks, dtype=jnp.int32), n_used - 1)
    expert_of = lambda b: jnp.minimum(jnp.sum((pend[None, :] <= (b * MOE_ROWS)[:, None]).astype(jnp.int32), axis=1),
                                      N_EXPERTS - 1)
    blk_exp = expert_of(blk)
    first = (blk == 0) | (blk_exp != expert_of(jnp.maximum(blk - 1, 0)))
    nxt = pend[blk_exp] // MOE_ROWS
    wraps = nxt >= n_used
    nxt_exp = jnp.where(wraps, expert_of(jnp.zeros_like(nxt)), expert_of(jnp.minimum(nxt, n_used - 1)))
    w_exp = jnp.where(first, blk_exp, nxt_exp)
    w_joff = jnp.where(first, 0, wraps.astype(jnp.int32))
    plan = (blk_exp, w_exp.astype(jnp.int32), w_joff.astype(jnp.int32), n_used.reshape(1).astype(jnp.int32))
    return slot_tok, pos.reshape(n, TOP_K), plan


def kernel(x_prompt, x_sample, state_hgrn, cache_k, cache_v, p_prompt, p_sample, norm_mix, norm_ffn, norm_ple, norm_final, a_w_in, a_w_out, a_lb_logits, a_gnorm, b_w_in, b_w_out, b_lambda_q1, b_lambda_k1, b_lambda_q2, b_lambda_k2, b_subln, router_group_w, router_group_b, router_expert_w, router_expert_b, expert_w_gate, expert_w_up, expert_w_down, ple_w_up, ple_w_gate):
    bp, tp, d = x_prompt.shape
    bs, ts, _ = x_sample.shape
    n_p, n_s = bp * tp, bs * ts
    depth = norm_mix.shape[0]
    nh_b = d // (2 * HEAD)

    h = (x_prompt.reshape(n_p, d), x_sample.reshape(n_s, d))
    lb_all = jnp.cumsum(jax.nn.softmax(a_lb_logits.astype(F32), axis=0), axis=0)
    states_p, states_s, k_p, v_p, k_s, v_s = [], [], [], [], [], []

    for i in range(depth):
        j = i // 2
        xn = _rmsnorm2(*h, norm_mix[i], BF16) if isinstance(h, tuple) else _rmsnorm(h, norm_mix[i], BF16)
        if i % 2 == 0:
            proj = _matmul(xn, a_w_in[j].astype(BF16))
            o_p, s_p = _hgrn2(proj, 0, bp, tp, lb_all[j], a_gnorm[j], None)
            o_s, s_s = _hgrn2(proj, n_p, bs, ts, lb_all[j], a_gnorm[j], state_hgrn[j])
            states_p.append(s_p)
            states_s.append(s_s)
            w_out = a_w_out[j]
        else:
            w_in = b_w_in[j]
            wk, wv = w_in[:, d:2 * d].astype(BF16), w_in[:, 2 * d:].astype(BF16)
            q = _matmul(xn, w_in[:, :d].astype(BF16), out_dtype=BF16, scale=HEAD ** -0.5 * LOG2E)
            (kp, kp16), vp = _matmul_rowsplit(xn, wk, rows=n_p), _matmul(xn, wv, rows=n_p)
            (ks, ks16), vs = _matmul_rowsplit(xn, wk, row0=n_p, rows=n_s), _matmul(xn, wv, row0=n_p, rows=n_s)
            lam0 = 0.8 - 0.6 * math.exp(-0.3 * i)
            lam = (jnp.exp(jnp.sum(b_lambda_q1[j] * b_lambda_k1[j]))
                   - jnp.exp(jnp.sum(b_lambda_q2[j] * b_lambda_k2[j])) + lam0).astype(F32)
            o_p = _attn_prompt(q, kp16, vp, lam, b_subln[j], 1.0 - lam0, bp, tp)
            o_s = _attn_sample(q, n_p, ks16, vs, cache_k[j], cache_v[j], lam, b_subln[j], 1.0 - lam0, bs, ts)
            k_p.append(kp.reshape(bp, tp, nh_b, 2, HEAD))
            v_p.append(vp.reshape(bp, tp, nh_b, 2 * HEAD))
            k_s.append(ks.reshape(bs, ts, nh_b, 2, HEAD))
            v_s.append(vs.reshape(bs, ts, nh_b, 2 * HEAD))
            w_out = b_w_out[j]
        h = _matmul_residual2(o_p, o_s, w_out.astype(BF16), h)

        w_router = jnp.zeros((d, ROUTER_LANES), F32)
        w_router = w_router.at[:, :N_GROUPS].set(router_group_w[i])
        w_router = w_router.at[:, N_GROUPS:N_GROUPS + N_EXPERTS].set(router_expert_w[i])
        b_router = jnp.zeros((ROUTER_LANES,), F32)
        b_router = b_router.at[:N_GROUPS].set(router_group_b[i].astype(F32))
        b_router = b_router.at[N_GROUPS:N_GROUPS + N_EXPERTS].set(router_expert_b[i].astype(F32))
        xq, expert_idx, gates = _rmsnorm_router(h, norm_ffn[i], w_router, b_router)
        slot_tok, pos, plan = _dispatch(expert_idx)
        yb = _moe_experts(_moe_dispatch(xq, slot_tok, plan[3]), plan, i, expert_w_gate, expert_w_up, expert_w_down)
        h, xn = _moe_combine(h, gates, yb, pos, norm_ple[i])

        p_i = jnp.concatenate([p_prompt[i].reshape(n_p, -1), p_sample[i].reshape(n_s, -1)], axis=0)
        h = _matmul_ple(xn, ple_w_gate[i].astype(BF16), h, p_i.astype(BF16), ple_w_up[i].astype(BF16))

    y_p = _rmsnorm(h, norm_final, F32, 0, n_p)
    y_s = _rmsnorm(h, norm_final, F32, n_p, n_s)
    return (y_p.reshape(bp, tp, d), y_s.reshape(bs, ts, d), jnp.stack(states_p), jnp.stack(states_s),
            jnp.stack(k_p), jnp.stack(v_p), jnp.stack(k_s), jnp.stack(v_s))
```

```python
import functools
import math

import jax
import jax.numpy as jnp
import numpy as np
from jax import lax
from jax.experimental import pallas as pl
from jax.experimental.pallas import tpu as pltpu

F32 = jnp.float32
BF16 = jnp.bfloat16

EPS = 1e-6
HEAD = 128
STREAM_CHUNK = 64
N_GROUPS = 4
EXPERTS_PER_GROUP = 8
N_EXPERTS = N_GROUPS * EXPERTS_PER_GROUP
TOP_K = 2
ROUTER_LANES = 128
MOE_ROWS = 256
RANK_GROUP = 512
HGRN2_HEADS_PER_STEP = 8
DMA_ISSUE_UNROLL = 8
ATTN_Q_ROWS = 512
V7X_VMEM_LIMIT = 56 * 1024 * 1024
LOG2E = math.log2(math.e)


def _tile(dim, want):
    t = min(dim, want)
    while dim % t:
        t //= 2
    return t


def _params(sem):
    return pltpu.CompilerParams(dimension_semantics=sem, vmem_limit_bytes=V7X_VMEM_LIMIT)


def _split3(x):
    hi = x.astype(BF16)
    r1 = x - hi.astype(F32)
    mid = r1.astype(BF16)
    lo = (r1 - mid.astype(F32)).astype(BF16)
    return hi, mid, lo


def _dot(a, b):
    return jnp.dot(a, b, preferred_element_type=F32)


def _dot_nt(a, b):
    return lax.dot_general(a, b, (((1,), (1,)), ((), ())), preferred_element_type=F32)


def _rmsnorm_kernel(x_ref, g_ref, o_ref):
    x = x_ref[...]
    y = x * lax.rsqrt(jnp.mean(x * x, axis=-1, keepdims=True) + EPS)
    o_ref[...] = (y * g_ref[...]).astype(o_ref.dtype)


def _rmsnorm(x, g, out_dtype, row0=0, rows=None):
    n, d = x.shape
    rows = n if rows is None else rows
    tm = _tile(math.gcd(rows, row0) if row0 else rows, 512)
    blk0 = row0 // tm
    return pl.pallas_call(
        _rmsnorm_kernel,
        out_shape=jax.ShapeDtypeStruct((rows, d), out_dtype),
        grid=(rows // tm,),
        in_specs=[pl.BlockSpec((tm, d), lambda i: (blk0 + i, 0)), pl.BlockSpec((1, d), lambda i: (0, 0))],
        out_specs=pl.BlockSpec((tm, d), lambda i: (i, 0)),
        compiler_params=_params(("parallel",)),
        name="rmsnorm",
    )(x, g.reshape(1, d))


def _rmsnorm2_kernel(xa_ref, xb_ref, g_ref, o_ref, *, a_tiles):
    i = pl.program_id(0)

    @pl.when(i < a_tiles)
    def _():
        _rmsnorm_kernel(xa_ref, g_ref, o_ref)

    @pl.when(i >= a_tiles)
    def _():
        _rmsnorm_kernel(xb_ref, g_ref, o_ref)


def _rmsnorm2(xa, xb, g, out_dtype):
    na, d = xa.shape
    nb = xb.shape[0]
    tm = _tile(math.gcd(na, nb), 512)
    a_tiles = na // tm
    return pl.pallas_call(
        functools.partial(_rmsnorm2_kernel, a_tiles=a_tiles),
        out_shape=jax.ShapeDtypeStruct((na + nb, d), out_dtype),
        grid=((na + nb) // tm,),
        in_specs=[pl.BlockSpec((tm, d), lambda i: (jnp.minimum(i, a_tiles - 1), 0)),
                  pl.BlockSpec((tm, d), lambda i: (jnp.maximum(i - a_tiles, 0), 0)),
                  pl.BlockSpec((1, d), lambda i: (0, 0))],
        out_specs=pl.BlockSpec((tm, d), lambda i: (i, 0)),
        compiler_params=_params(("parallel",)),
        name="rmsnorm",
    )(xa, xb, g.reshape(1, d))


def _first_lane(cond, lane):
    return jnp.min(jnp.where(cond, lane, float(ROUTER_LANES)), axis=-1, keepdims=True)


def _route_lanes(logits):
    lane = lax.broadcasted_iota(jnp.int32, logits.shape, 1).astype(F32)
    is_group = lane < N_GROUPS

    def softmax_over(mask):
        z = jnp.where(mask, logits, -jnp.inf)
        e = jnp.exp(z - jnp.max(z, axis=-1, keepdims=True))
        return e / jnp.sum(e, axis=-1, keepdims=True)

    pg = softmax_over(is_group)
    pg_top = jnp.max(pg, axis=-1, keepdims=True)
    g_top = _first_lane((pg == pg_top) & is_group, lane)
    lo = N_GROUPS + EXPERTS_PER_GROUP * g_top
    in_group = (lane >= lo) & (lane < lo + EXPERTS_PER_GROUP)
    pe = jnp.where(in_group, softmax_over(in_group), -1.0)
    p1 = jnp.max(pe, axis=-1, keepdims=True)
    e1 = _first_lane(pe == p1, lane)
    pe_rest = jnp.where(lane == e1, -1.0, pe)
    p2 = jnp.max(pe_rest, axis=-1, keepdims=True)
    e2 = _first_lane(pe_rest == p2, lane)
    scale = pg_top / (p1 + p2)
    out = jnp.where(lane == 0, p1 * scale, jnp.where(lane == 1, p2 * scale, 0.0))
    return jnp.where(lane == 2, e1 - N_GROUPS, jnp.where(lane == 3, e2 - N_GROUPS, out))


def _rmsnorm_router_kernel(x_ref, g_ref, w0_ref, w1_ref, w2_ref, b_ref, o_ref, r_ref):
    x = x_ref[...]
    y = x * lax.rsqrt(jnp.mean(x * x, axis=-1, keepdims=True) + EPS)
    xn = y * g_ref[...]
    o_ref[...] = _pack_halves(xn)
    x0, x1, x2 = _split3(xn)
    w0, w1, w2 = w0_ref[...], w1_ref[...], w2_ref[...]
    small = _dot(x0, w2) + _dot(x1, w1) + _dot(x2, w0)
    mid = _dot(x0, w1) + _dot(x1, w0)
    r_ref[...] = _route_lanes(_dot(x0, w0) + (mid + small) + b_ref[...])


def _rmsnorm_router(x, g, w_router, b_router):
    n, d = x.shape
    tm = _tile(n, 512)
    w0, w1, w2 = _split3(w_router)
    wspec = pl.BlockSpec((d, ROUTER_LANES), lambda i: (0, 0))
    xq, route = pl.pallas_call(
        _rmsnorm_router_kernel,
        out_shape=(jax.ShapeDtypeStruct((n, d // 2), jnp.uint32), jax.ShapeDtypeStruct((n, ROUTER_LANES), F32)),
        grid=(n // tm,),
        in_specs=[pl.BlockSpec((tm, d), lambda i: (i, 0)), pl.BlockSpec((1, d), lambda i: (0, 0)),
                  wspec, wspec, wspec, pl.BlockSpec((1, ROUTER_LANES), lambda i: (0, 0))],
        out_specs=(pl.BlockSpec((tm, d // 2), lambda i: (i, 0)), pl.BlockSpec((tm, ROUTER_LANES), lambda i: (i, 0))),
        compiler_params=_params(("parallel",)),
        name="rmsnorm_router",
    )(x, g.reshape(1, d), w0, w1, w2, b_router.reshape(1, ROUTER_LANES))
    return xq, route[:, TOP_K:2 * TOP_K].astype(jnp.int32), route[:, :TOP_K]


def _mm_kernel(x_ref, w_ref, o_ref, *, scale):
    acc = _dot(x_ref[...], w_ref[...])
    if scale is not None:
        acc = acc * scale
    o_ref[...] = acc.astype(o_ref.dtype)


def _mm_rowsplit_kernel(x_ref, w_ref, o_ref, o16_ref):
    acc = _dot(x_ref[...], w_ref[...])
    o16_ref[...] = acc.astype(o16_ref.dtype)
    for c in range(o_ref.shape[1]):
        o_ref[:, c, :] = acc[:, c * HEAD:(c + 1) * HEAD]


def _matmul_rowsplit(x, w, *, row0=0, rows=None, col0=0, cols=None, tm=1024, tn=1024):
    m, k = x.shape
    rows = m if rows is None else rows
    n = w.shape[1] if cols is None else cols
    tm = _tile(math.gcd(rows, row0) if row0 else rows, tm)
    tn = _tile(math.gcd(n, col0) if col0 else n, tn)
    blk0, cblk0 = row0 // tm, col0 // tn
    return pl.pallas_call(
        _mm_rowsplit_kernel,
        out_shape=(jax.ShapeDtypeStruct((rows, n // HEAD, HEAD), F32), jax.ShapeDtypeStruct((rows, n), BF16)),
        grid=(rows // tm, n // tn),
        in_specs=[pl.BlockSpec((tm, k), lambda i, j: (blk0 + i, 0)),
                  pl.BlockSpec((k, tn), lambda i, j: (0, cblk0 + j))],
        out_specs=(pl.BlockSpec((tm, tn // HEAD, HEAD), lambda i, j: (i, j, 0)),
                   pl.BlockSpec((tm, tn), lambda i, j: (i, j))),
        compiler_params=_params(("parallel", "arbitrary")),
        name="matmul_rowsplit",
    )(x, w)


def _mm_res2_kernel(xa_ref, xb_ref, w_ref, ra_ref, rb_ref, o_ref, *, a_tiles):
    i = pl.program_id(0)

    @pl.when(i < a_tiles)
    def _():
        o_ref[...] = ra_ref[...] + _dot(xa_ref[...], w_ref[...])

    @pl.when(i >= a_tiles)
    def _():
        o_ref[...] = rb_ref[...] + _dot(xb_ref[...], w_ref[...])


def _mm_ple_kernel(x_ref, w_ref, r_ref, p_ref, wup_ref, o_ref):
    gate = jax.nn.sigmoid(_dot(x_ref[...], w_ref[...]))
    o_ref[...] = r_ref[...] + _dot(p_ref[...], wup_ref[...]) * gate


def _cast_kernel(x_ref, o_ref):
    o_ref[...] = x_ref[...].astype(o_ref.dtype)


def _to_bf16(w, layer):
    _, k, n = w.shape
    tk, tn = _tile(k, 512), _tile(n, 2048)
    return pl.pallas_call(
        _cast_kernel,
        out_shape=jax.ShapeDtypeStruct((k, n), BF16),
        grid=(k // tk, n // tn),
        in_specs=[pl.BlockSpec((None, tk, tn), lambda i, j: (layer, i, j))],
        out_specs=pl.BlockSpec((tk, tn), lambda i, j: (i, j)),
        compiler_params=_params(("parallel", "parallel")),
        name="cast_bf16",
    )(w)


def _matmul(x, w, *, out_dtype=F32, scale=None, row0=0, rows=None, col0=0, cols=None, tm=1024, tn=1024):
    m, k = x.shape
    rows = m if rows is None else rows
    n = w.shape[1] if cols is None else cols
    tm = _tile(math.gcd(rows, row0) if row0 else rows, tm)
    tn = _tile(math.gcd(n, col0) if col0 else n, tn)
    blk0, cblk0 = row0 // tm, col0 // tn
    return pl.pallas_call(
        functools.partial(_mm_kernel, scale=scale),
        out_shape=jax.ShapeDtypeStruct((rows, n), out_dtype),
        grid=(rows // tm, n // tn),
        in_specs=[pl.BlockSpec((tm, k), lambda i, j: (blk0 + i, 0)),
                  pl.BlockSpec((k, tn), lambda i, j: (0, cblk0 + j))],
        out_specs=pl.BlockSpec((tm, tn), lambda i, j: (i, j)),
        compiler_params=_params(("parallel", "arbitrary")),
        name="matmul",
    )(x, w)


def _matmul_residual2(xa, xb, w, res):
    ma, k = xa.shape
    mb = xb.shape[0]
    n = w.shape[1]
    tm, tn = _tile(math.gcd(ma, mb), 1024), _tile(n, 512)
    a_tiles, b_tiles = ma // tm, mb // tm
    in_a = lambda i: jnp.minimum(i, a_tiles - 1)
    in_b = lambda i: jnp.maximum(i - a_tiles, 0)
    if isinstance(res, tuple):
        b_off = 0
    else:
        res, b_off = (res, res), a_tiles
    return pl.pallas_call(
        functools.partial(_mm_res2_kernel, a_tiles=a_tiles),
        out_shape=jax.ShapeDtypeStruct((ma + mb, n), F32),
        grid=(a_tiles + b_tiles, n // tn),
        in_specs=[pl.BlockSpec((tm, k), lambda i, j: (in_a(i), 0)),
                  pl.BlockSpec((tm, k), lambda i, j: (in_b(i), 0)),
                  pl.BlockSpec((k, tn), lambda i, j: (0, j)),
                  pl.BlockSpec((tm, tn), lambda i, j: (in_a(i), jnp.where(i < a_tiles, j, n // tn - 1))),
                  pl.BlockSpec((tm, tn), lambda i, j: (b_off + in_b(i), jnp.where(i < a_tiles, 0, j)))],
        out_specs=pl.BlockSpec((tm, tn), lambda i, j: (i, j)),
        compiler_params=_params(("parallel", "arbitrary")),
        name="matmul_residual",
    )(xa, xb, w, *res)


def _matmul_ple(xn, w_gate, res, p, w_up, *, tm=1024, tn=512):
    m, k = xn.shape
    n = w_gate.shape[1]
    kp = p.shape[1]
    tm, tn = _tile(m, tm), _tile(n, tn)
    return pl.pallas_call(
        _mm_ple_kernel,
        out_shape=jax.ShapeDtypeStruct((m, n), F32),
        grid=(m // tm, n // tn),
        in_specs=[pl.BlockSpec((tm, k), lambda i, j: (i, 0)), pl.BlockSpec((k, tn), lambda i, j: (0, j)),
                  pl.BlockSpec((tm, tn), lambda i, j: (i, j)),
                  pl.BlockSpec((tm, kp), lambda i, j: (i, 0)), pl.BlockSpec((kp, tn), lambda i, j: (0, j))],
        out_specs=pl.BlockSpec((tm, tn), lambda i, j: (i, j)),
        compiler_params=_params(("parallel", "arbitrary")),
        name="matmul_ple",
    )(xn, w_gate, res, p, w_up)


def _hgrn2_level_table(c):
    t = np.arange(c)[:, None]
    s = np.arange(c)[None, :]
    x = np.bitwise_xor(t, s)
    lvl = np.where(x > 0, np.floor(np.log2(np.maximum(x, 1))).astype(np.int32), -1)
    return np.where(s < t, lvl, -1).astype(np.int32)


def _hgrn2_sign_table(c):
    t = np.arange(c)[None, :, None]
    j = np.arange(int(math.log2(c)))[:, None, None]
    return np.broadcast_to(np.where((t >> j) & 1, 1.0, -1.0), (j.shape[0], c, HEAD)).astype(np.float32)


def _hgrn2_kernel(q_ref, f_ref, v_ref, g_ref, lb_ref, gn_ref, lvl_ref, tri_ref, sgn_ref, *rest, chunk, n_chunks, hp,
                  has_s0):
    if has_s0:
        s0_ref, o_ref, s_ref, st_ref, b_ref = rest
    else:
        o_ref, s_ref, st_ref, b_ref = rest
    tb = pl.program_id(2)
    c = chunk

    @pl.when(tb == 0)
    def _():
        for hh in range(hp):
            st_ref[hh] = s0_ref[0, hh].T if has_s0 else jnp.zeros((HEAD, HEAD), F32)

    gn = gn_ref[...]
    lvl = lvl_ref[...]
    tri = tri_ref[...]
    sub = lax.broadcasted_iota(jnp.int32, (8, HEAD), 0)
    n_levels = int(math.log2(c))

    def gates(rows, hh):
        lanes = slice(hh * HEAD, (hh + 1) * HEAD)
        lb = lb_ref[:, lanes]
        q = jax.nn.silu(q_ref[rows, lanes])
        f = lb + (1.0 - lb) * jax.nn.sigmoid(f_ref[rows, lanes])
        k = 1.0 - f
        lg = jnp.log(f) * LOG2E
        l0, l1, l2 = _split3(lg)
        cs = _dot(tri, jnp.concatenate([l0, l1, l2], axis=1))
        b = cs[:, :HEAD] + (cs[:, HEAD:2 * HEAD] + cs[:, 2 * HEAD:])
        b_ref[hh] = b
        att = jnp.where(lvl == 0, _dot_nt((q * f).astype(BF16), k.astype(BF16)), 0.0)
        return q, k, b, att

    def level(j, hh, q, k, b, att):
        m = 1 << j
        pieces = []
        for g8 in range(c // 8):
            if m >= 4:
                r = (g8 * 8 // (2 * m)) * 2 * m + m - 1
                pieces.append(jnp.broadcast_to(b_ref[hh, r:r + 1, :], (8, HEAD)))
            else:
                top = jnp.broadcast_to(b_ref[hh, g8 * 8 + 1:g8 * 8 + 2, :], (8, HEAD))
                bot = jnp.broadcast_to(b_ref[hh, g8 * 8 + 5:g8 * 8 + 6, :], (8, HEAD))
                pieces.append(jnp.where(sub < 4, top, bot))
        e = jnp.exp2((b - jnp.concatenate(pieces, axis=0)) * sgn_ref[j])
        sj = _dot_nt((q * e).astype(BF16), (k * e).astype(BF16))
        return jnp.where(lvl == j, sj, att)

    def finish(rows, hh, q, k, b, att):
        lanes = slice(hh * HEAD, (hh + 1) * HEAD)
        v = v_ref[rows, lanes]
        st = st_ref[hh]
        b_end = b_ref[hh, c - 1:c, :]
        o = _dot(att.astype(BF16), v.astype(BF16))
        o = o + _dot_nt((q * jnp.exp2(b)).astype(BF16), st.astype(BF16))
        o = o + jnp.sum(q * k, axis=-1, keepdims=True) * v
        kd = (k * jnp.exp2(b_end - b)).astype(BF16)
        st_ref[hh] = jnp.exp2(b_end) * st + _dot(v.T.astype(BF16), kd)
        o = o * lax.rsqrt(jnp.mean(o * o, axis=-1, keepdims=True) + EPS) * gn
        o_ref[rows, lanes] = (o * jax.nn.silu(g_ref[rows, lanes])).astype(o_ref.dtype)

    def one_chunk(ci, carry):
        rows = pl.ds(pl.multiple_of(ci * c, c), c)
        work = [gates(rows, hh) for hh in range(hp)]
        for j in range(1, n_levels):
            work = [w[:3] + (level(j, hh, *w),) for hh, w in enumerate(work)]
        for hh, w in enumerate(work):
            finish(rows, hh, *w)
        return carry

    lax.fori_loop(0, n_chunks, one_chunk, 0)

    @pl.when(tb == pl.num_programs(2) - 1)
    def _():
        for hh in range(hp):
            s_ref[0, hh] = st_ref[hh].T


def _hgrn2(proj, row_off, bsz, t, lb, gnorm, s0):
    d = proj.shape[1] // 4
    nh = d // HEAD
    hp = _tile(nh, HGRN2_HEADS_PER_STEP)
    c = min(t, 128)
    n_levels = int(math.log2(c))
    tb = _tile(t, 512)
    n_tb = t // tb
    assert row_off % tb == 0 and tb % c == 0
    blk0 = row_off // tb
    ng = nh // hp
    col = lambda kind: pl.BlockSpec((tb, hp * HEAD), lambda b, h, i: (blk0 + b * n_tb + i, kind * ng + h))
    const = lambda shape: pl.BlockSpec(shape, lambda b, h, i: (0, 0))
    st_spec = pl.BlockSpec((1, hp, HEAD, HEAD), lambda b, h, i: (b, h, 0, 0))
    in_specs = [col(0), col(1), col(2), col(3), pl.BlockSpec((1, hp * HEAD), lambda b, h, i: (0, h)),
                const((1, HEAD)), const((c, c)), const((c, c)),
                pl.BlockSpec((n_levels, c, HEAD), lambda b, h, i: (0, 0, 0))]
    args = [proj, proj, proj, proj, lb.reshape(1, d), gnorm.reshape(1, HEAD),
            jnp.asarray(_hgrn2_level_table(c)), jnp.asarray(np.tril(np.ones((c, c), np.float32)), BF16),
            jnp.asarray(_hgrn2_sign_table(c))]
    if s0 is not None:
        in_specs.append(st_spec)
        args.append(s0)
    return pl.pallas_call(
        functools.partial(_hgrn2_kernel, chunk=c, n_chunks=tb // c, hp=hp, has_s0=s0 is not None),
        out_shape=(jax.ShapeDtypeStruct((bsz * t, d), BF16), jax.ShapeDtypeStruct((bsz, nh, HEAD, HEAD), F32)),
        grid=(bsz, ng, n_tb),
        in_specs=in_specs,
        out_specs=(pl.BlockSpec((tb, hp * HEAD), lambda b, h, i: (b * n_tb + i, h)), st_spec),
        scratch_shapes=[pltpu.VMEM((hp, HEAD, HEAD), F32), pltpu.VMEM((hp, c, HEAD), F32)],
        compiler_params=_params(("parallel", "parallel", "arbitrary")),
        name="hgrn2_scan",
    )(*args)


def _subln(o, g, post_scale):
    return o * lax.rsqrt(jnp.mean(o * o, axis=-1, keepdims=True) + EPS) * g * post_scale


def _attn_prompt_kernel(lam_ref, q_ref, kb_ref, v_ref, g_ref, o_ref, vb_ref, *, tq, n_q, post_scale):
    qi = pl.program_id(2)

    @pl.when(qi == 0)
    def _():
        vb_ref[...] = v_ref[...].astype(BF16)

    lam = lam_ref[0]
    q = q_ref[...]
    for vi in range(n_q):
        @pl.when(qi == vi)
        def _(head=vi * tq, ext=(vi + 1) * tq):
            pos = lax.broadcasted_iota(jnp.int32, (tq, tq), 0)
            allowed = lax.broadcasted_iota(jnp.int32, (tq, tq), 1) <= (pos | (STREAM_CHUNK - 1))
            v_tail = vb_ref[head:ext, :]

            def one_map(c0):
                qc = q[:, c0:c0 + HEAD]
                st = jnp.where(allowed, _dot_nt(qc, kb_ref[head:ext, c0:c0 + HEAD]), -jnp.inf)
                m = jnp.max(st, axis=-1, keepdims=True)
                if head:
                    sh = _dot_nt(qc, kb_ref[0:head, c0:c0 + HEAD])
                    m = jnp.maximum(m, jnp.max(sh, axis=-1, keepdims=True))
                et = jnp.exp2(st - m)
                tot = jnp.sum(et, axis=-1, keepdims=True)
                acc = _dot(et.astype(BF16), v_tail)
                if head:
                    eh = jnp.exp2(sh - m)
                    tot = tot + jnp.sum(eh, axis=-1, keepdims=True)
                    acc = acc + _dot(eh.astype(BF16), vb_ref[0:head, :])
                return acc / tot

            o = one_map(0) - lam * one_map(HEAD)
            o_ref[...] = _subln(o, g_ref[...], post_scale).astype(o_ref.dtype)


def _attn_prompt(q, k, v, lam, subln, post_scale, bsz, t):
    d = q.shape[1]
    hw = 2 * HEAD
    nh = d // hw
    tq = _tile(t, ATTN_Q_ROWS)
    n_q = t // tq
    return pl.pallas_call(
        functools.partial(_attn_prompt_kernel, tq=tq, n_q=n_q, post_scale=post_scale),
        out_shape=jax.ShapeDtypeStruct((bsz * t, d), BF16),
        grid=(bsz, nh, n_q),
        in_specs=[pl.BlockSpec(memory_space=pltpu.SMEM),
                  pl.BlockSpec((tq, hw), lambda b, h, i: (b * n_q + i, h)),
                  pl.BlockSpec((t, hw), lambda b, h, i: (b, h)),
                  pl.BlockSpec((t, hw), lambda b, h, i: (b, h)),
                  pl.BlockSpec((1, hw), lambda b, h, i: (0, 0))],
        out_specs=pl.BlockSpec((tq, hw), lambda b, h, i: (b * n_q + i, h)),
        scratch_shapes=[pltpu.VMEM((t, hw), BF16)],
        compiler_params=_params(("parallel", "parallel", "arbitrary")),
        name="diff_attn_prompt",
    )(lam.reshape(1), q, k, v, subln.reshape(1, hw))


def _attn_sample_kernel(lam_ref, q_ref, kn_ref, vn_ref, ck_ref, cv_ref, g_ref, o_ref, *, hg, post_scale):
    hw = 2 * HEAD
    lam = lam_ref[0]
    past = ck_ref.shape[0]
    ck_rows = ck_ref.reshape(past * 2 * hg, HEAD)
    for h in range(hg):
        lanes = slice(h * hw, (h + 1) * hw)
        v = cv_ref[:, lanes].astype(BF16)
        vn = vn_ref[:, lanes].astype(BF16)
        outs = []
        for c in range(2):
            col = slice((2 * h + c) * HEAD, (2 * h + c + 1) * HEAD)
            qc = q_ref[:, col]
            s = _dot_nt(qc, ck_rows[pl.ds(2 * h + c, past, stride=2 * hg), :].astype(BF16))
            sn = _dot_nt(qc, kn_ref[:, col])
            m = jnp.maximum(jnp.max(s, axis=-1, keepdims=True), jnp.max(sn, axis=-1, keepdims=True))
            e, en = jnp.exp2(s - m), jnp.exp2(sn - m)
            tot = jnp.sum(e, axis=-1, keepdims=True) + jnp.sum(en, axis=-1, keepdims=True)
            outs.append((_dot(e.astype(BF16), v) + _dot(en.astype(BF16), vn)) / tot)
        o = outs[0] - lam * outs[1]
        o_ref[:, lanes] = _subln(o, g_ref[...], post_scale).astype(o_ref.dtype)


def _attn_sample(q, q_row0, k, v, cache_k, cache_v, lam, subln, post_scale, bsz, t):
    d = q.shape[1]
    hw = 2 * HEAD
    nh = d // hw
    past = cache_k.shape[1]
    hg = _tile(nh, 4)
    assert (2 * hg) % 8 == 0 or hg == nh
    assert q_row0 % t == 0
    blk0 = q_row0 // t
    new = pl.BlockSpec((t, hg * hw), lambda b, g: (b, g))
    return pl.pallas_call(
        functools.partial(_attn_sample_kernel, hg=hg, post_scale=post_scale),
        out_shape=jax.ShapeDtypeStruct((bsz * t, d), BF16),
        grid=(bsz, nh // hg),
        in_specs=[pl.BlockSpec(memory_space=pltpu.SMEM),
                  pl.BlockSpec((t, hg * hw), lambda b, g: (blk0 + b, g)), new, new,
                  pl.BlockSpec((past, 2 * hg, HEAD), lambda b, g: (b, g, 0)),
                  pl.BlockSpec((past, hg * hw), lambda b, g: (b, g)),
                  pl.BlockSpec((1, hw), lambda b, g: (0, 0))],
        out_specs=pl.BlockSpec((t, hg * hw), lambda b, g: (b, g)),
        compiler_params=_params(("parallel", "parallel")),
        name="diff_attn_sample",
    )(lam.reshape(1), q, k, v, cache_k.reshape(bsz * past, 2 * nh, HEAD), cache_v.reshape(bsz * past, d),
      subln.reshape(1, hw))


def _expert_changed(be_ref):
    b = pl.program_id(1)
    return (b == 0) | (be_ref[b] != be_ref[jnp.maximum(b - 1, 0)])


def _pack_halves(x):
    half = x.shape[1] // 2
    as_bits = lambda v: lax.bitcast_convert_type(v.astype(BF16).astype(F32), jnp.uint32)
    return (as_bits(x[:, :half]) >> 16) | (as_bits(x[:, half:]) & jnp.uint32(0xFFFF0000))


def _unpack_halves(u):
    lo = lax.bitcast_convert_type(u << 16, F32).astype(BF16)
    hi = lax.bitcast_convert_type(u & jnp.uint32(0xFFFF0000), F32).astype(BF16)
    return lo, hi


def _moe_dispatch_kernel(tok_ref, nu_ref, x_hbm, o_ref, buf_ref, sem_ref, *, rows):
    i = pl.program_id(0)
    n_used = nu_ref[0]

    def copy(tile, r, slot):
        return pltpu.make_async_copy(x_hbm.at[tok_ref[tile * rows + r]], buf_ref.at[slot, r], sem_ref.at[slot])

    def start_tile(tile, slot):
        lax.fori_loop(0, rows, lambda r, c: (copy(tile, r, slot).start(), c)[1], 0, unroll=DMA_ISSUE_UNROLL)

    @pl.when(i == 0)
    def _():
        start_tile(0, 0)

    @pl.when(i + 1 < n_used)
    def _():
        start_tile(i + 1, (i + 1) % 2)

    @pl.when(i < n_used)
    def _():
        slot = i % 2
        pltpu.make_async_copy(x_hbm.at[pl.ds(0, rows)], buf_ref.at[slot], sem_ref.at[slot]).wait()
        o_ref[...] = buf_ref[slot]

    @pl.when(i >= n_used)
    def _():
        o_ref[...] = jnp.zeros(o_ref.shape, o_ref.dtype)


def _moe_dispatch(xq, slot_tok, n_used):
    w = xq.shape[1]
    p = slot_tok.shape[0]
    rows = MOE_ROWS
    return pl.pallas_call(
        functools.partial(_moe_dispatch_kernel, rows=rows),
        out_shape=jax.ShapeDtypeStruct((p, w), xq.dtype),
        grid_spec=pltpu.PrefetchScalarGridSpec(
            num_scalar_prefetch=2, grid=(p // rows,),
            in_specs=[pl.BlockSpec(memory_space=pl.ANY)],
            out_specs=pl.BlockSpec((rows, w), lambda i, tok, nu: (i, 0)),
            scratch_shapes=[pltpu.VMEM((2, rows, w), xq.dtype), pltpu.SemaphoreType.DMA((2,))]),
        compiler_params=_params(("arbitrary",)),
        name="moe_dispatch",
    )(slot_tok, n_used, xq)


def _moe_combine_kernel(pos_ref, h_ref, g_ref, gn_ref, yb_hbm, o_ref, on_ref, buf_ref, sem_ref, *, rows):
    i = pl.program_id(0)
    n_tiles = pl.num_programs(0)

    def copy(tile, r, k, slot):
        return pltpu.make_async_copy(yb_hbm.at[pos_ref[(tile * rows + r) * TOP_K + k]], buf_ref.at[slot, k, r],
                                     sem_ref.at[slot])

    def start_tile(tile, slot):
        def body(r, carry):
            for k in range(TOP_K):
                copy(tile, r, k, slot).start()
            return carry
        lax.fori_loop(0, rows, body, 0, unroll=DMA_ISSUE_UNROLL)

    @pl.when(i == 0)
    def _():
        start_tile(0, 0)

    @pl.when(i + 1 < n_tiles)
    def _():
        start_tile(i + 1, (i + 1) % 2)

    slot = i % 2
    for k in range(TOP_K):
        pltpu.make_async_copy(yb_hbm.at[pl.ds(0, rows)], buf_ref.at[slot, k], sem_ref.at[slot]).wait()
    g = g_ref[...]
    y = buf_ref[slot, 0] * g[:, 0:1]
    for k in range(1, TOP_K):
        y = y + buf_ref[slot, k] * g[:, k:k + 1]
    out = h_ref[...] + y
    o_ref[...] = out
    on = out * lax.rsqrt(jnp.mean(out * out, axis=-1, keepdims=True) + EPS)
    on_ref[...] = (on * gn_ref[...]).astype(on_ref.dtype)


def _moe_combine(h, gates, yb, pos, g_next):
    n, d = h.shape
    rows = _tile(n, 128)
    tile = pl.BlockSpec((rows, d), lambda i, pos: (i, 0))
    return pl.pallas_call(
        functools.partial(_moe_combine_kernel, rows=rows),
        out_shape=(jax.ShapeDtypeStruct((n, d), F32), jax.ShapeDtypeStruct((n, d), BF16)),
        grid_spec=pltpu.PrefetchScalarGridSpec(
            num_scalar_prefetch=1, grid=(n // rows,),
            in_specs=[tile, pl.BlockSpec((rows, TOP_K), lambda i, pos: (i, 0)),
                      pl.BlockSpec((1, d), lambda i, pos: (0, 0)), pl.BlockSpec(memory_space=pl.ANY)],
            out_specs=(tile, tile),
            scratch_shapes=[pltpu.VMEM((2, TOP_K, rows, d), F32), pltpu.SemaphoreType.DMA((2,))]),
        compiler_params=_params(("arbitrary",)),
        name="moe_combine",
    )(pos.reshape(-1), h, gates, g_next.reshape(1, d), yb)


def _moe_up_kernel(be_ref, we_ref, wj_ref, nu_ref, x_ref, wg_ref, wu_ref, o_ref, wgb_ref, wub_ref):
    del we_ref, wj_ref

    @pl.when(pl.program_id(1) < nu_ref[0])
    def _():
        @pl.when(_expert_changed(be_ref))
        def _():
            wgb_ref[...] = wg_ref[...].astype(BF16)
            wub_ref[...] = wu_ref[...].astype(BF16)

        xl, xh = _unpack_halves(x_ref[...])
        half = xl.shape[1]
        g = _dot(xl, wgb_ref[0:half, :]) + _dot(xh, wgb_ref[half:, :])
        u = _dot(xl, wub_ref[0:half, :]) + _dot(xh, wub_ref[half:, :])
        o_ref[...] = (jax.nn.silu(g) * u).astype(o_ref.dtype)

    @pl.when(pl.program_id(1) >= nu_ref[0])
    def _():
        o_ref[...] = jnp.zeros(o_ref.shape, o_ref.dtype)


def _moe_down_kernel(be_ref, we_ref, wj_ref, nu_ref, h_ref, wd_ref, o_ref, wdb_ref):
    del we_ref, wj_ref

    @pl.when(pl.program_id(1) < nu_ref[0])
    def _():
        @pl.when(_expert_changed(be_ref))
        def _():
            wdb_ref[...] = wd_ref[...].astype(BF16)

        o_ref[...] = _dot(h_ref[...], wdb_ref[...])

    @pl.when(pl.program_id(1) >= nu_ref[0])
    def _():
        o_ref[...] = jnp.zeros(o_ref.shape, o_ref.dtype)


def _moe_experts(xs, plan, layer, w_gate, w_up, w_down):
    blk_exp, w_exp, w_joff, n_used = plan
    p, w = xs.shape
    d = 2 * w
    de = w_gate.shape[3]
    nb = p // MOE_ROWS
    te = _tile(de, 512)
    row = lambda j, b, be, we, wj, nu: (jnp.minimum(b, nu[0] - 1), 0)
    out = lambda j, b, be, we, wj, nu: (b, j)

    def weight(n_j):
        return lambda j, b, be, we, wj, nu: (layer, we[b], 0, jnp.minimum(j + wj[b], n_j - 1))

    hidden = pl.pallas_call(
        _moe_up_kernel,
        out_shape=jax.ShapeDtypeStruct((p, de), BF16),
        grid_spec=pltpu.PrefetchScalarGridSpec(
            num_scalar_prefetch=4, grid=(de // te, nb),
            in_specs=[pl.BlockSpec((MOE_ROWS, w), row),
                      pl.BlockSpec((None, None, d, te), weight(de // te)),
                      pl.BlockSpec((None, None, d, te), weight(de // te))],
            out_specs=pl.BlockSpec((MOE_ROWS, te), out),
            scratch_shapes=[pltpu.VMEM((d, te), BF16), pltpu.VMEM((d, te), BF16)]),
        compiler_params=_params(("arbitrary", "arbitrary")),
        name="moe_gate_up",
    )(blk_exp, w_exp, w_joff, n_used, xs, w_gate, w_up)
    tn = _tile(d, 2048)
    return pl.pallas_call(
        _moe_down_kernel,
        out_shape=jax.ShapeDtypeStruct((p, d), F32),
        grid_spec=pltpu.PrefetchScalarGridSpec(
            num_scalar_prefetch=4, grid=(d // tn, nb),
            in_specs=[pl.BlockSpec((MOE_ROWS, de), row),
                      pl.BlockSpec((None, None, de, tn), weight(d // tn))],
            out_specs=pl.BlockSpec((MOE_ROWS, tn), out),
            scratch_shapes=[pltpu.VMEM((de, tn), BF16)]),
        compiler_params=_params(("arbitrary", "arbitrary")),
        name="moe_down",
    )(blk_exp, w_exp, w_joff, n_used, hidden, w_down)


def _slot_rank_kernel(e_ref, tri_ref, rank_ref, count_ref, carry_ref):
    i = pl.program_id(0)

    @pl.when(i == 0)
    def _():
        carry_ref[...] = jnp.zeros(carry_ref.shape, F32)

    onehot = e_ref[...] == lax.broadcasted_iota(jnp.int32, (e_ref.shape[0], ROUTER_LANES), 1)
    seen = _dot(tri_ref[...], jnp.where(onehot, 1.0, 0.0).astype(BF16)) + carry_ref[...]
    rank_ref[...] = (jnp.sum(jnp.where(onehot, seen, 0.0), axis=-1, keepdims=True) - 1.0).astype(jnp.int32)
    carry_ref[...] = seen[-1:, :]

    @pl.when(i == pl.num_programs(0) - 1)
    def _():
        count_ref[...] = seen[-1:, :].astype(jnp.int32)


def _slot_rank(flat_e):
    a = flat_e.shape[0]
    g = _tile(a, RANK_GROUP)
    rank, count = pl.pallas_call(
        _slot_rank_kernel,
        out_shape=(jax.ShapeDtypeStruct((a, 1), jnp.int32), jax.ShapeDtypeStruct((1, ROUTER_LANES), jnp.int32)),
        grid=(a // g,),
        in_specs=[pl.BlockSpec((g, 1), lambda i: (i, 0)), pl.BlockSpec((g, g), lambda i: (0, 0))],
        out_specs=(pl.BlockSpec((g, 1), lambda i: (i, 0)), pl.BlockSpec((1, ROUTER_LANES), lambda i: (0, 0))),
        scratch_shapes=[pltpu.VMEM((1, ROUTER_LANES), F32)],
        compiler_params=_params(("arbitrary",)),
        name="moe_slot_rank",
    )(flat_e.reshape(a, 1), jnp.asarray(np.tril(np.ones((g, g), np.float32)), BF16))
    return rank[:, 0], count[0, :N_EXPERTS]


def _dispatch(expert_idx):
    n = expert_idx.shape[0]
    a = n * TOP_K
    flat_e = expert_idx.reshape(-1).astype(jnp.int32)
    rank, counts = _slot_rank(flat_e)
    padded = (counts + MOE_ROWS - 1) // MOE_ROWS * MOE_ROWS
    pend = jnp.cumsum(padded)
    pstart = pend - padded
    n_blocks = -(-(a + N_EXPERTS * (MOE_ROWS - 1)) // MOE_ROWS)
    pos = pstart.at[flat_e].get(mode="promise_in_bounds") + rank
    slot_tok = jnp.zeros((n_blocks * MOE_ROWS,), jnp.int32).at[pos].set(
        jnp.arange(a, dtype=jnp.int32) // TOP_K, mode="promise_in_bounds", unique_indices=True)

    n_used = pend[-1] // MOE_ROWS
    blk = jnp.minimum(jnp.arange(n_blocks, dtype=jnp.int32), n_used - 1)
    expert_of = lambda b: jnp.minimum(jnp.sum((pend[None, :] <= (b * MOE_ROWS)[:, None]).astype(jnp.int32), axis=1),
                                      N_EXPERTS - 1)
    blk_exp = expert_of(blk)
    first = (blk == 0) | (blk_exp != expert_of(jnp.maximum(blk - 1, 0)))
    nxt = pend[blk_exp] // MOE_ROWS
    wraps = nxt >= n_used
    nxt_exp = jnp.where(wraps, expert_of(jnp.zeros_like(nxt)), expert_of(jnp.minimum(nxt, n_used - 1)))
    w_exp = jnp.where(first, blk_exp, nxt_exp)
    w_joff = jnp.where(first, 0, wraps.astype(jnp.int32))
    plan = (blk_exp, w_exp.astype(jnp.int32), w_joff.astype(jnp.int32), n_used.reshape(1).astype(jnp.int32))
    return slot_tok, pos.reshape(n, TOP_K), plan


def kernel(x_prompt, x_sample, state_hgrn, cache_k, cache_v, p_prompt, p_sample, norm_mix, norm_ffn, norm_ple, norm_final, a_w_in, a_w_out, a_lb_logits, a_gnorm, b_w_in, b_w_out, b_lambda_q1, b_lambda_k1, b_lambda_q2, b_lambda_k2, b_subln, router_group_w, router_group_b, router_expert_w, router_expert_b, expert_w_gate, expert_w_up, expert_w_down, ple_w_up, ple_w_gate):
    bp, tp, d = x_prompt.shape
    bs, ts, _ = x_sample.shape
    n_p, n_s = bp * tp, bs * ts
    depth = norm_mix.shape[0]
    nh_b = d // (2 * HEAD)

    h = (x_prompt.reshape(n_p, d), x_sample.reshape(n_s, d))
    lb_all = jnp.cumsum(jax.nn.softmax(a_lb_logits.astype(F32), axis=0), axis=0)
    states_p, states_s, k_p, v_p, k_s, v_s = [], [], [], [], [], []

    for i in range(depth):
        j = i // 2
        xn = _rmsnorm2(*h, norm_mix[i], BF16) if isinstance(h, tuple) else _rmsnorm(h, norm_mix[i], BF16)
        if i % 2 == 0:
            proj = _matmul(xn, _to_bf16(a_w_in, j))
            o_p, s_p = _hgrn2(proj, 0, bp, tp, lb_all[j], a_gnorm[j], None)
            o_s, s_s = _hgrn2(proj, n_p, bs, ts, lb_all[j], a_gnorm[j], state_hgrn[j])
            states_p.append(s_p)
            states_s.append(s_s)
            w_out = _to_bf16(a_w_out, j)
        else:
            w_in = _to_bf16(b_w_in, j)
            q = _matmul(xn, w_in, cols=d, out_dtype=BF16, scale=HEAD ** -0.5 * LOG2E)
            kp, kp16 = _matmul_rowsplit(xn, w_in, rows=n_p, col0=d, cols=d)
            ks, ks16 = _matmul_rowsplit(xn, w_in, row0=n_p, rows=n_s, col0=d, cols=d)
            vp = _matmul(xn, w_in, rows=n_p, col0=2 * d, cols=d)
            vs = _matmul(xn, w_in, row0=n_p, rows=n_s, col0=2 * d, cols=d)
            lam0 = 0.8 - 0.6 * math.exp(-0.3 * i)
            lam = (jnp.exp(jnp.sum(b_lambda_q1[j] * b_lambda_k1[j]))
                   - jnp.exp(jnp.sum(b_lambda_q2[j] * b_lambda_k2[j])) + lam0).astype(F32)
            o_p = _attn_prompt(q, kp16, vp, lam, b_subln[j], 1.0 - lam0, bp, tp)
            o_s = _attn_sample(q, n_p, ks16, vs, cache_k[j], cache_v[j], lam, b_subln[j], 1.0 - lam0, bs, ts)
            k_p.append(kp.reshape(bp, tp, nh_b, 2, HEAD))
            v_p.append(vp.reshape(bp, tp, nh_b, 2 * HEAD))
            k_s.append(ks.reshape(bs, ts, nh_b, 2, HEAD))
            v_s.append(vs.reshape(bs, ts, nh_b, 2 * HEAD))
            w_out = _to_bf16(b_w_out, j)
        h = _matmul_residual2(o_p, o_s, w_out, h)

        w_router = jnp.zeros((d, ROUTER_LANES), F32)
        w_router = w_router.at[:, :N_GROUPS].set(router_group_w[i])
        w_router = w_router.at[:, N_GROUPS:N_GROUPS + N_EXPERTS].set(router_expert_w[i])
        b_router = jnp.zeros((ROUTER_LANES,), F32)
        b_router = b_router.at[:N_GROUPS].set(router_group_b[i].astype(F32))
        b_router = b_router.at[N_GROUPS:N_GROUPS + N_EXPERTS].set(router_expert_b[i].astype(F32))
        xq, expert_idx, gates = _rmsnorm_router(h, norm_ffn[i], w_router, b_router)
        slot_tok, pos, plan = _dispatch(expert_idx)
        yb = _moe_experts(_moe_dispatch(xq, slot_tok, plan[3]), plan, i, expert_w_gate, expert_w_up, expert_w_down)
        h, xn = _moe_combine(h, gates, yb, pos, norm_ple[i])

        p_i = jnp.concatenate([p_prompt[i].reshape(n_p, -1), p_sample[i].reshape(n_s, -1)], axis=0)
        h = _matmul_ple(xn, _to_bf16(ple_w_gate, i), h, p_i.astype(BF16), ple_w_up[i].astype(BF16))

    y_p = _rmsnorm(h, norm_final, F32, 0, n_p)
    y_s = _rmsnorm(h, norm_final, F32, n_p, n_s)
    return (y_p.reshape(bp, tp, d), y_s.reshape(bs, ts, d), jnp.stack(states_p), jnp.stack(states_s),
            jnp.stack(k_p), jnp.stack(v_p), jnp.stack(k_s), jnp.stack(v_s))
```
